```python
import jax, jax.numpy as jnp
from jax import lax
import numpy as np

D_MODEL = 1024
BATCH = 1
SEQ = 16384
DEPTH = 1
DEC_BATCH = 128
DEC_SEQ = 8
PAST_LEN = 16384
PAGE_SIZE = 128

MLA_HEADS = 8
MLA_NOPE = 64
MLA_ROPE = 32
MLA_V = 64
Q_LORA = 384
KV_LORA = 256
MOBA_HEADS = 8
MOBA_KV_HEADS = 4
MOBA_GROUP = MOBA_HEADS // MOBA_KV_HEADS
HEAD_DIM = 64
MOBA_BLOCK = 256
MOBA_TOPK = 3
N_GROUPS = 4
EXPERTS_PER_GROUP = 8
N_EXPERTS = N_GROUPS * EXPERTS_PER_GROUP
EXPERT_TOP_K = 2
D_EXPERT = 512
ROPE_THETA = 10000.0
NORM_EPS = 1e-6
Q_BLOCK = 128
IN_SIZES = (Q_LORA, KV_LORA, MLA_ROPE, MOBA_HEADS * HEAD_DIM, MOBA_KV_HEADS * HEAD_DIM, MOBA_KV_HEADS * HEAD_DIM, D_MODEL, D_MODEL)
D_IN = sum(IN_SIZES)

kernel_name = 'hybrid_mla_moba_hmoe_step'


def rmsnorm(x, g):
    xf = x.astype(jnp.float32)
    y = xf * lax.rsqrt(jnp.mean(xf * xf, axis=-1, keepdims=True) + NORM_EPS)
    return (y * g.astype(jnp.float32)).astype(x.dtype)


def rope(x, pos):
    half = x.shape[-1] // 2
    inv = ROPE_THETA ** (-jnp.arange(half, dtype=jnp.float32) / half)
    ang = pos.astype(jnp.float32)[:, None] * inv[None, :]
    cos = jnp.cos(ang)[:, None, :]
    sin = jnp.sin(ang)[:, None, :]
    x1 = x[..., :half].astype(jnp.float32)
    x2 = x[..., half:].astype(jnp.float32)
    return jnp.concatenate([x1 * cos - x2 * sin, x2 * cos + x1 * sin], axis=-1).astype(x.dtype)


def token_features(h, pos, w_in, g_q_lora, g_kv_lora, w_uq, g_q_nope, g_q_rope, g_k_rope, g_moba_q, g_moba_k):
    B, T, _ = h.shape
    z = h @ w_in
    cuts, acc = [], 0
    for s in IN_SIZES[:-1]:
        acc += s
        cuts.append(acc)
    cq, ckv, kr, mq, mk, mv, ga, gb = jnp.split(z, cuts, axis=-1)
    cq = rmsnorm(cq, g_q_lora)
    ckv = rmsnorm(ckv, g_kv_lora)
    q = (cq @ w_uq).reshape(B, T, MLA_HEADS, MLA_NOPE + MLA_ROPE)
    q_nope = rmsnorm(q[..., :MLA_NOPE], g_q_nope)
    q_rope = rope(rmsnorm(q[..., MLA_NOPE:], g_q_rope), pos)
    k_rope = rope(rmsnorm(kr, g_k_rope)[:, :, None, :], pos)[:, :, 0]
    mq = rope(rmsnorm(mq.reshape(B, T, MOBA_HEADS, HEAD_DIM), g_moba_q), pos)
    mk = rope(rmsnorm(mk.reshape(B, T, MOBA_KV_HEADS, HEAD_DIM), g_moba_k), pos)
    mv = mv.reshape(B, T, MOBA_KV_HEADS, HEAD_DIM)
    return q_nope, q_rope, ckv, k_rope, mq, mk, mv, ga, gb


def mla_latent_to_kv(ckv, w_uk, w_uv, g_k_nope):
    k_nope = rmsnorm(jnp.einsum('lr,rhd->lhd', ckv, w_uk), g_k_nope)
    v = jnp.einsum('lr,rhd->lhd', ckv, w_uv)
    return k_nope, v


def mla_attend(q_nope, q_rope, pos_q, k_nope, k_rope, v, pos_k):
    scale = (MLA_NOPE + MLA_ROPE) ** -0.5
    s = jnp.einsum('thd,lhd->htl', q_nope, k_nope) + jnp.einsum('thd,ld->htl', q_rope, k_rope)
    s = s.astype(jnp.float32) * scale
    s = jnp.where((pos_k[None, :] <= pos_q[:, None])[None], s, -jnp.inf)
    p = jax.nn.softmax(s, axis=-1).astype(v.dtype)
    return jnp.einsum('htl,lhd->thd', p, v)


def moba_blocks(k, v):
    L = k.shape[0]
    nb = -(-L // MOBA_BLOCK)
    pad = nb * MOBA_BLOCK - L
    kp = jnp.pad(k, ((0, pad), (0, 0), (0, 0)))
    vp = jnp.pad(v, ((0, pad), (0, 0), (0, 0)))
    kb = kp.reshape(nb, MOBA_BLOCK, MOBA_KV_HEADS, HEAD_DIM).transpose(2, 0, 1, 3)
    vb = vp.reshape(nb, MOBA_BLOCK, MOBA_KV_HEADS, HEAD_DIM).transpose(2, 0, 1, 3)
    means = jnp.mean(kb.astype(jnp.float32), axis=2).astype(k.dtype)
    return kb, vb, means


def moba_attend(q, pos_q, k_blocks, v_blocks, means):
    Tq = q.shape[0]
    nb = k_blocks.shape[1]
    k_sel = min(MOBA_TOPK, nb)
    scale = HEAD_DIM ** -0.5
    cur = pos_q // MOBA_BLOCK
    s_blk = jnp.einsum('thgd,hnd->thgn', q, means).astype(jnp.float32)
    past = jnp.arange(nb)[None, :] < cur[:, None]
    s_blk = jnp.where(past[:, None, None, :], s_blk, -jnp.inf)
    _, idx = lax.top_k(s_blk, k_sel)
    valid = jnp.arange(k_sel)[None, :] < cur[:, None]
    hk = jnp.arange(MOBA_KV_HEADS)[None, :, None, None]
    k_g = k_blocks[hk, idx]
    v_g = v_blocks[hk, idx]
    k_own = k_blocks[:, cur]
    v_own = v_blocks[:, cur]
    s_sel = jnp.einsum('thgd,thgkbd->thgkb', q, k_g).astype(jnp.float32) * scale
    s_sel = jnp.where(valid[:, None, None, :, None], s_sel, -jnp.inf)
    s_own = jnp.einsum('thgd,htbd->thgb', q, k_own).astype(jnp.float32) * scale
    own_pos = cur[:, None] * MOBA_BLOCK + jnp.arange(MOBA_BLOCK)[None, :]
    s_own = jnp.where((own_pos <= pos_q[:, None])[:, None, None, :], s_own, -jnp.inf)
    n_sel = k_sel * MOBA_BLOCK
    s = jnp.concatenate([s_sel.reshape(Tq, MOBA_KV_HEADS, MOBA_GROUP, n_sel), s_own], axis=-1)
    p = jax.nn.softmax(s, axis=-1).astype(v_blocks.dtype)
    p_sel = p[..., :n_sel].reshape(Tq, MOBA_KV_HEADS, MOBA_GROUP, k_sel, MOBA_BLOCK)
    p_own = p[..., n_sel:]
    return jnp.einsum('thgkb,thgkbd->thgd', p_sel, v_g) + jnp.einsum('thgb,htbd->thgd', p_own, v_own)


def hier_moe(h, w_group, b_group, w_expert_router, b_expert_router, w_exp_gate, w_exp_up, w_exp_down):
    T = h.shape[0]
    g_logits = (h @ w_group + b_group).astype(jnp.float32)
    g_prob = jax.nn.softmax(g_logits, axis=-1)
    _, g_idx = lax.top_k(g_logits, 1)
    p_g = jnp.take_along_axis(g_prob, g_idx, axis=-1)
    e_logits = (h @ w_expert_router + b_expert_router).astype(jnp.float32).reshape(T, N_GROUPS, EXPERTS_PER_GROUP)
    e_in = e_logits[jnp.arange(T), g_idx[:, 0]]
    top_v, top_i = lax.top_k(e_in, EXPERT_TOP_K)
    w_sel = jax.nn.softmax(top_v, axis=-1) * p_g
    expert_id = g_idx * EXPERTS_PER_GROUP + top_i
    gates = jnp.sum(jax.nn.one_hot(expert_id, N_EXPERTS, dtype=jnp.float32) * w_sel[..., None], axis=1).astype(h.dtype)

    def body(acc, ew):
        wg, wu, wd, ge = ew
        hid = jax.nn.silu(h @ wg) * (h @ wu)
        return acc + ge[:, None] * (hid @ wd), None

    out, _ = lax.scan(body, jnp.zeros_like(h), (w_exp_gate, w_exp_up, w_exp_down, gates.T))
    return out


def merge_and_ffn(x, oa, ob, ga, gb, w_oa, w_ob, w_out, g_ffn, moe_w):
    mix = jax.nn.sigmoid(ga) * (oa @ w_oa) + jax.nn.sigmoid(gb) * (ob @ w_ob)
    x1 = x + mix @ w_out
    B, T, D = x.shape
    ff = hier_moe(rmsnorm(x1, g_ffn).reshape(B * T, D), *moe_w).reshape(B, T, D)
    return x1 + ff


def setup_inputs(seed: int = 0) -> dict:
    key = jax.random.key(seed)
    ks = jax.random.split(key, 32)
    f32 = jnp.float32

    def nrm(i, shape, scale):
        return jax.random.normal(ks[i], shape, f32) * scale

    def gain(i, n):
        return 1.0 + 0.02 * jax.random.normal(ks[i], (n,), f32)

    n_pages = PAST_LEN // PAGE_SIZE
    n_pool = (5 * DEC_BATCH * n_pages) // 4
    perm = jax.random.permutation(ks[6], n_pool)[:DEC_BATCH * n_pages]
    page_table = perm.reshape(DEC_BATCH, n_pages).astype(jnp.int32)
    return {
        'x_prompt': nrm(0, (BATCH, SEQ, D_MODEL), 1.0),
        'x_sample': nrm(1, (DEC_BATCH, DEC_SEQ, D_MODEL), 1.0),
        'cache_ckv': nrm(2, (n_pool, PAGE_SIZE, KV_LORA), 1.0),
        'cache_krope': nrm(3, (n_pool, PAGE_SIZE, MLA_ROPE), 1.0),
        'cache_k': nrm(4, (n_pool, PAGE_SIZE, MOBA_KV_HEADS, HEAD_DIM), 1.0),
        'cache_v': nrm(5, (n_pool, PAGE_SIZE, MOBA_KV_HEADS, HEAD_DIM), 1.0),
        'page_table': page_table,
        'g_attn': gain(7, D_MODEL),
        'w_in': nrm(8, (D_MODEL, D_IN), D_MODEL ** -0.5),
        'g_q_lora': gain(9, Q_LORA),
        'g_kv_lora': gain(10, KV_LORA),
        'w_uq': nrm(11, (Q_LORA, MLA_HEADS * (MLA_NOPE + MLA_ROPE)), Q_LORA ** -0.5),
        'g_q_nope': gain(12, MLA_NOPE),
        'g_q_rope': gain(13, MLA_ROPE),
        'g_k_rope': gain(14, MLA_ROPE),
        'w_uk': nrm(15, (KV_LORA, MLA_HEADS, MLA_NOPE), KV_LORA ** -0.5),
        'w_uv': nrm(16, (KV_LORA, MLA_HEADS, MLA_V), KV_LORA ** -0.5),
        'g_k_nope': gain(17, MLA_NOPE),
        'g_moba_q': gain(18, HEAD_DIM),
        'g_moba_k': gain(19, HEAD_DIM),
        'w_oa': nrm(20, (MLA_HEADS * MLA_V, D_MODEL), (MLA_HEADS * MLA_V) ** -0.5),
        'w_ob': nrm(21, (MOBA_HEADS * HEAD_DIM, D_MODEL), (MOBA_HEADS * HEAD_DIM) ** -0.5),
        'w_out': nrm(22, (D_MODEL, D_MODEL), D_MODEL ** -0.5),
        'g_ffn': gain(23, D_MODEL),
        'w_group': nrm(24, (D_MODEL, N_GROUPS), D_MODEL ** -0.5),
        'b_group': nrm(25, (N_GROUPS,), 0.01),
        'w_expert_router': nrm(26, (D_MODEL, N_EXPERTS), D_MODEL ** -0.5),
        'b_expert_router': nrm(27, (N_EXPERTS,), 0.01),
        'w_exp_gate': nrm(28, (N_EXPERTS, D_MODEL, D_EXPERT), D_MODEL ** -0.5),
        'w_exp_up': nrm(29, (N_EXPERTS, D_MODEL, D_EXPERT), D_MODEL ** -0.5),
        'w_exp_down': nrm(30, (N_EXPERTS, D_EXPERT, D_MODEL), D_EXPERT ** -0.5),
    }


def reference(x_prompt, x_sample, cache_ckv, cache_krope, cache_k, cache_v, page_table, g_attn, w_in, g_q_lora, g_kv_lora, w_uq, g_q_nope, g_q_rope, g_k_rope, w_uk, w_uv, g_k_nope, g_moba_q, g_moba_k, w_oa, w_ob, w_out, g_ffn, w_group, b_group, w_expert_router, b_expert_router, w_exp_gate, w_exp_up, w_exp_down):
    feat_w = (w_in, g_q_lora, g_kv_lora, w_uq, g_q_nope, g_q_rope, g_k_rope, g_moba_q, g_moba_k)
    moe_w = (w_group, b_group, w_expert_router, b_expert_router, w_exp_gate, w_exp_up, w_exp_down)
    y_prompt, y_sample = x_prompt, x_sample
    for _layer in range(DEPTH):
        B, T, _ = y_prompt.shape
        pos_p = jnp.arange(T, dtype=jnp.int32)
        h_p = rmsnorm(y_prompt, g_attn)
        qn_p, qr_p, ckv_p, kr_p, mq_p, mk_p, mv_p, ga_p, gb_p = token_features(h_p, pos_p, *feat_w)
        nqb = T // Q_BLOCK

        def prompt_seq(args):
            qn, qr, ckv, kr, mq, mk, mv = args
            kn, v = mla_latent_to_kv(ckv, w_uk, w_uv, g_k_nope)

            def mla_block(a):
                qn_b, qr_b, pos_b = a
                return mla_attend(qn_b, qr_b, pos_b, kn, kr, v, pos_p)

            oa = lax.map(mla_block, (qn.reshape(nqb, Q_BLOCK, MLA_HEADS, MLA_NOPE), qr.reshape(nqb, Q_BLOCK, MLA_HEADS, MLA_ROPE), pos_p.reshape(nqb, Q_BLOCK)))
            kb, vb, means = moba_blocks(mk, mv)

            def moba_block(a):
                q_b, pos_b = a
                return moba_attend(q_b, pos_b, kb, vb, means)

            ob = lax.map(moba_block, (mq.reshape(nqb, Q_BLOCK, MOBA_KV_HEADS, MOBA_GROUP, HEAD_DIM), pos_p.reshape(nqb, Q_BLOCK)))
            return oa.reshape(T, MLA_HEADS * MLA_V), ob.reshape(T, MOBA_HEADS * HEAD_DIM)

        oa_p, ob_p = lax.map(prompt_seq, (qn_p, qr_p, ckv_p, kr_p, mq_p, mk_p, mv_p))
        y_prompt = merge_and_ffn(y_prompt, oa_p, ob_p, ga_p, gb_p, w_oa, w_ob, w_out, g_ffn, moe_w)

        Ts = y_sample.shape[1]
        past = page_table.shape[1] * PAGE_SIZE
        pos_s = past + jnp.arange(Ts, dtype=jnp.int32)
        pos_k = jnp.arange(past + Ts, dtype=jnp.int32)
        h_s = rmsnorm(y_sample, g_attn)
        qn_s, qr_s, ckv_s, kr_s, mq_s, mk_s, mv_s, ga_s, gb_s = token_features(h_s, pos_s, *feat_w)

        def sample_seq(args):
            qn, qr, ckv_new, kr_new, mq, mk_new, mv_new, pages = args
            ckv_all = jnp.concatenate([cache_ckv[pages].reshape(past, KV_LORA), ckv_new], axis=0)
            kr_all = jnp.concatenate([cache_krope[pages].reshape(past, MLA_ROPE), kr_new], axis=0)
            kn, v = mla_latent_to_kv(ckv_all, w_uk, w_uv, g_k_nope)
            oa = mla_attend(qn, qr, pos_s, kn, kr_all, v, pos_k)
            k_all = jnp.concatenate([cache_k[pages].reshape(past, MOBA_KV_HEADS, HEAD_DIM), mk_new], axis=0)
            v_all = jnp.concatenate([cache_v[pages].reshape(past, MOBA_KV_HEADS, HEAD_DIM), mv_new], axis=0)
            kb, vb, means = moba_blocks(k_all, v_all)
            ob = moba_attend(mq.reshape(Ts, MOBA_KV_HEADS, MOBA_GROUP, HEAD_DIM), pos_s, kb, vb, means)
            return oa.reshape(Ts, MLA_HEADS * MLA_V), ob.reshape(Ts, MOBA_HEADS * HEAD_DIM)

        oa_s, ob_s = lax.map(sample_seq, (qn_s, qr_s, ckv_s, kr_s, mq_s, mk_s, mv_s, page_table))
        y_sample = merge_and_ffn(y_sample, oa_s, ob_s, ga_s, gb_s, w_oa, w_ob, w_out, g_ffn, moe_w)
    return (y_prompt, y_sample, ckv_p, kr_p, mk_p, mv_p, ckv_s, kr_s, mk_s, mv_s)
```

```python
import functools
import math

import jax
import jax.numpy as jnp
from jax import lax
from jax.experimental import pallas as pl
from jax.experimental.pallas import tpu as pltpu

MLA_HEADS = 8
MLA_NOPE = 64
MLA_ROPE = 32
MLA_V = 64
MOBA_HEADS = 8
MOBA_KV_HEADS = 4
MOBA_GROUP = MOBA_HEADS // MOBA_KV_HEADS
HEAD_DIM = 64
MOBA_BLOCK = 256
MOBA_TOPK = 3
N_GROUPS = 4
EXPERTS_PER_GROUP = 8
N_EXPERTS = N_GROUPS * EXPERTS_PER_GROUP
ROPE_THETA = 10000.0
NORM_EPS = 1e-6

LANE = 128
QPAD = LANE
LOG2E = 1.4426950408889634
NEG = -1e30
VMEM_LIMIT = 56 * 1024 * 1024

F32 = jnp.float32
BF16 = jnp.bfloat16


def _dot(a, b):
    return jnp.dot(a, b, preferred_element_type=F32)


def _dot_nt(a, b):
    return lax.dot_general(a, b, (((1,), (1,)), ((), ())), preferred_element_type=F32)


def _split_bf16(x):
    hi = x.astype(BF16)
    lo = (x - hi.astype(F32)).astype(BF16)
    return hi, lo


def _group_ssq(x, g):
    hi, lo = _split_bf16(x * x)
    return _dot(hi, g) + _dot(lo, g)


def _rms(x):
    return x * lax.rsqrt(jnp.mean(x * x, axis=-1, keepdims=True) + NORM_EPS)


def _tile_lanes(x, n):
    return x if n == 1 else jnp.concatenate([x] * n, axis=1)


def _feat_kernel(prompt, dims, *refs):
    ql, kvl = dims
    (x_ref, gattn, w_in, gql, gkvl, gmq, gmqs, gmk, gmks, gkr, gkrs,
     c64, s64, c32, s32, cqt, sqt, g512, g128, g1024, icq, wuqa, wuqb, gqa, gqb) = refs[:25]
    rest = refs[25:]
    x = x_ref[...]
    h = (_rms(x) * gattn[...]).astype(BF16)
    z = _dot(h, w_in[...])
    o = 0
    cq = z[:, o:o + ql]; o += ql
    ckv = z[:, o:o + kvl]; o += kvl
    mq = z[:, o:o + 512]; o += 512
    mqs = z[:, o:o + 512]; o += 512
    mk = z[:, o:o + 256]; o += 256
    mks = z[:, o:o + 256]; o += 256
    mv = z[:, o:o + 256]; o += 256
    kr = z[:, o:o + 128]; o += 128
    krs = z[:, o:o + 128]; o += 128

    cqn = _rms(cq) * gql[...]
    ckvn = _rms(ckv) * gkvl[...]

    r = lax.rsqrt(_group_ssq(kr, g128[...]) * (1.0 / MLA_ROPE) + NORM_EPS)
    krn = r * (kr * gkr[...] * c32[...] + krs * gkrs[...] * s32[...])

    c2 = _tile_lanes(c64[...], 2)
    s2 = _tile_lanes(s64[...], 2)
    r = lax.rsqrt(_group_ssq(mk, g512[:256, :256]) * (1.0 / HEAD_DIM) + NORM_EPS)
    mkn = r * (mk * gmk[...] * c2 + mks * gmks[...] * s2)

    c4 = _tile_lanes(c2, 2)
    s4 = _tile_lanes(s2, 2)
    r = lax.rsqrt(_group_ssq(mq, g512[...]) * (1.0 / HEAD_DIM) + NORM_EPS)
    mqn = r * (mq * gmq[...] * c4 + mqs * gmqs[...] * s4)

    cqb = cqn.astype(BF16)
    qa = _dot(cqb, wuqa[...])
    qb = _dot(cqb, wuqb[...])
    r = lax.rsqrt(_group_ssq(qa, g1024[...]) * icq[...] + NORM_EPS)
    c8 = _tile_lanes(cqt[...], MLA_HEADS)
    s8 = _tile_lanes(sqt[...], MLA_HEADS)
    qn = r * (qa * gqa[...] * c8 + qb * gqb[...] * s8)

    if prompt:
        wuk, gkn, ekr, wuv = rest[:4]
        ckv_o, kr_o, mk_o, mv_o, qt_o, kp_o, vt_o, mqt_o, mkb_o, mvt_o = rest[4:]
    else:
        ckv_o, kr_o, mk_o, mv_o, q_o, mq_o = rest
    ckv_o[...] = ckvn
    kr_o[...] = krn[:, :MLA_ROPE]
    mk_o[...] = mkn
    mv_o[...] = mv
    if prompt:
        ckvb = ckvn.astype(BF16)
        kraw = _dot(ckvb, wuk[...])
        rk = lax.rsqrt(_group_ssq(kraw, g1024[...]) * icq[...] + NORM_EPS)
        kp = rk * kraw * gkn[...] + _dot(krn.astype(BF16), ekr[...])
        kp_o[...] = kp.astype(BF16)
        qt_o[...] = qn.T.astype(BF16)
        vt_o[...] = _dot(ckvb, wuv[...]).T.astype(BF16)
        mqt_o[...] = mqn.T.astype(BF16)
        mkb_o[...] = mkn.astype(BF16)
        mvt_o[...] = mv.T.astype(BF16)
    else:
        q_o[...] = qn
        mq_o[...] = mqn


def _swap_halves(w, hd):
    lead = w.shape[:-1]
    n = w.shape[-1] // hd
    w = w.reshape(lead + (n, 2, hd // 2))
    return jnp.flip(w, axis=-2).reshape(lead + (n * hd,))


def _pad_lanes(w, width):
    return jnp.pad(w, [(0, 0)] * (w.ndim - 1) + [(0, width - w.shape[-1])])


def _group_matrix(width, starts_sizes, period):
    lane = jnp.arange(width)
    blk = lane // period
    off = lane % period
    gid = jnp.full((width,), -1, jnp.int32)
    for k, (st, sz) in enumerate(starts_sizes):
        gid = jnp.where((off >= st) & (off < st + sz), k, gid)
    same = (blk[:, None] == blk[None, :]) & (gid[:, None] == gid[None, :]) & (gid[:, None] >= 0)
    return same.astype(BF16)


def _rope_tables(pos, half):
    inv = ROPE_THETA ** (-jnp.arange(half, dtype=F32) / half)
    ang = pos.astype(F32)[:, None] * inv[None, :]
    return jnp.cos(ang), jnp.sin(ang)


def _features(x2d, pos, w, prompt, tm):
    t, d = x2d.shape
    ql = w["g_q_lora"].shape[0]
    kvl = w["g_kv_lora"].shape[0]
    assert t % tm == 0
    cos32, sin32 = _rope_tables(pos, HEAD_DIM // 2)
    cos16, sin16 = _rope_tables(pos, MLA_ROPE // 2)
    c64 = jnp.concatenate([cos32, cos32] * 2, axis=1)
    s64 = jnp.concatenate([-sin32, sin32] * 2, axis=1)
    z96 = jnp.zeros((t, LANE - MLA_ROPE), F32)
    c32 = jnp.concatenate([cos16, cos16, z96], axis=1)
    s32 = jnp.concatenate([-sin16, sin16, z96], axis=1)
    z32 = jnp.zeros((t, QPAD - MLA_NOPE - MLA_ROPE), F32)
    cqt = jnp.concatenate([jnp.ones((t, MLA_NOPE), F32), cos16, cos16, z32], axis=1)
    sqt = jnp.concatenate([jnp.zeros((t, MLA_NOPE), F32), -sin16, sin16, z32], axis=1)

    row_args = [x2d, w["gattn"], w["w_in_p" if prompt else "w_in_s"], w["gql"], w["gkvl"],
                w["gmq"], w["gmqs"], w["gmk"], w["gmks"], w["gkr"], w["gkrs"]]
    tab_args = [c64, s64, c32, s32, cqt, sqt]
    mat_args = [w["g512"], w["g128"], w["g1024"], w["icq"], w["wuqa"], w["wuqb"], w["gqa"], w["gqb"]]
    if prompt:
        mat_args += [w["wuk_pad"], w["gkn"], w["ekr"], w["wuv"]]

    def full(a):
        return pl.BlockSpec(a.shape, lambda i: (0,) * a.ndim)

    in_specs = [pl.BlockSpec((tm, d), lambda i: (i, 0))] + [full(a) for a in row_args[1:]]
    in_specs += [pl.BlockSpec((tm, LANE), lambda i: (i, 0)) for _ in tab_args]
    in_specs += [full(a) for a in mat_args]

    def rows(width, dtype):
        return jax.ShapeDtypeStruct((t, width), dtype), pl.BlockSpec((tm, width), lambda i: (i, 0))

    def cols(height, dtype):
        return jax.ShapeDtypeStruct((height, t), dtype), pl.BlockSpec((height, tm), lambda i: (0, i))

    outs = [rows(kvl, F32), rows(MLA_ROPE, F32), rows(256, F32), rows(256, F32)]
    if prompt:
        outs += [cols(MLA_HEADS * QPAD, BF16), rows(MLA_HEADS * QPAD, BF16), cols(MLA_HEADS * MLA_V, BF16),
                 cols(512, BF16), rows(256, BF16), cols(256, BF16)]
    else:
        outs += [rows(MLA_HEADS * QPAD, F32), rows(512, F32)]
    return pl.pallas_call(
        functools.partial(_feat_kernel, prompt, (ql, kvl)),
        grid=(t // tm,),
        in_specs=in_specs,
        out_specs=[o[1] for o in outs],
        out_shape=[o[0] for o in outs],
        compiler_params=pltpu.CompilerParams(dimension_semantics=("parallel",), vmem_limit_bytes=VMEM_LIMIT),
        name="features_prompt" if prompt else "features_sample",
    )(*row_args, *tab_args, *mat_args)


def _softmax_step(s, vt, m, l, acc):
    m_new = jnp.maximum(m, jnp.max(s, axis=0, keepdims=True))
    alpha = jnp.exp2(m - m_new)
    p = jnp.exp2(s - m_new)
    l = alpha * l + jnp.sum(p, axis=0, keepdims=True)
    acc = alpha * acc + _dot(vt, p.astype(BF16))
    return m_new, l, acc


def _mla_prompt_kernel(tq, tk, qt_ref, kp_ref, vt_ref, o_ref):
    i = pl.program_id(1)
    q = qt_ref[...]
    dv = vt_ref.shape[0]
    init = (jnp.full((1, tq), NEG, F32), jnp.zeros((1, tq), F32), jnp.zeros((dv, tq), F32))
    per = tq // tk

    def tile(j, carry, masked):
        off = pl.multiple_of(j * tk, tk)
        s = _dot(kp_ref[pl.ds(off, tk), :], q)
        if masked:
            kpos = off + lax.broadcasted_iota(jnp.int32, (tk, tq), 0)
            qpos = i * tq + lax.broadcasted_iota(jnp.int32, (tk, tq), 1)
            s = jnp.where(kpos <= qpos, s, NEG)
        return _softmax_step(s, vt_ref[:, pl.ds(off, tk)], *carry)

    carry = lax.fori_loop(0, i * per, lambda j, c: tile(j, c, False), init)
    carry = lax.fori_loop(i * per, (i + 1) * per, lambda j, c: tile(j, c, True), carry)
    _, l, acc = carry
    o_ref[...] = acc / l


def _mla_prompt(qt, kp, vt, tq, tk):
    t = kp.shape[0]
    nh = MLA_HEADS
    return pl.pallas_call(
        functools.partial(_mla_prompt_kernel, tq, tk),
        grid=(nh, t // tq),
        in_specs=[pl.BlockSpec((QPAD, tq), lambda h, i: (h, i)),
                  pl.BlockSpec((t, QPAD), lambda h, i: (0, h)),
                  pl.BlockSpec((MLA_V, t), lambda h, i: (h, 0))],
        out_specs=pl.BlockSpec((MLA_V, tq), lambda h, i: (h, i)),
        out_shape=jax.ShapeDtypeStruct((nh * MLA_V, t), F32),
        compiler_params=pltpu.CompilerParams(dimension_semantics=("parallel", "arbitrary"),
                                             vmem_limit_bytes=VMEM_LIMIT),
        name="mla_prompt",
    )(qt, kp, vt)


def _block_mean_kernel(k_ref, o_ref):
    k = k_ref[...]
    n = o_ref.shape[0]
    o_ref[...] = jnp.mean(k.reshape(n, MOBA_BLOCK, k.shape[1]), axis=1)


def _block_means(k2d):
    t, wdt = k2d.shape
    nb = t // MOBA_BLOCK
    per = 8 if nb % 8 == 0 else nb
    return pl.pallas_call(
        _block_mean_kernel,
        grid=(nb // per,),
        in_specs=[pl.BlockSpec((per * MOBA_BLOCK, wdt), lambda i: (i, 0))],
        out_specs=pl.BlockSpec((per, wdt), lambda i: (i, 0)),
        out_shape=jax.ShapeDtypeStruct((nb, wdt), F32),
        compiler_params=pltpu.CompilerParams(dimension_semantics=("parallel",)),
        name="moba_block_means",
    )(k2d)


def _select_topk_rows(s, limit):
    nb = s.shape[0]
    row = lax.broadcasted_iota(jnp.int32, s.shape, 0)
    s = jnp.where(row < limit, s, NEG)
    sel = jnp.zeros(s.shape, F32)
    for _ in range(MOBA_TOPK):
        mx = jnp.max(s, axis=0, keepdims=True)
        first = jnp.min(jnp.where(s == mx, row, nb), axis=0, keepdims=True)
        hit = (row == first) & (mx > 0.5 * NEG)
        sel = jnp.where(hit, 1.0, sel)
        s = jnp.where(row == first, NEG, s)
    return sel


def _moba_prompt_kernel(tq, q_ref, k_ref, vt_ref, mean_ref, o_ref, sel_ref):
    h = pl.program_id(0)
    i = pl.program_id(1)
    tk = MOBA_BLOCK
    odd = (h // MOBA_GROUP) % 2
    q64 = q_ref[...]
    zero = jnp.zeros_like(q64)
    q = jnp.concatenate([jnp.where(odd == 0, q64, zero), jnp.where(odd == 1, q64, zero)], axis=0)
    qpos1 = i * tq + lax.broadcasted_iota(jnp.int32, (1, tq), 1)
    cur1 = qpos1 // MOBA_BLOCK
    mhi, mlo = _split_bf16(mean_ref[...])
    sel_ref[...] = _select_topk_rows(_dot(mhi, q) + _dot(mlo, q), cur1)

    dv = vt_ref.shape[0]
    init = (jnp.full((1, tq), NEG, F32), jnp.zeros((1, tq), F32), jnp.zeros((dv, tq), F32))
    per = tq // tk

    def tile(n, carry, own):
        off = pl.multiple_of(n * tk, tk)
        s = _dot(k_ref[pl.ds(off, tk), :], q)
        picked = jnp.where(sel_ref[pl.ds(n, 1), :] > 0.0, s, NEG)
        if own:
            kpos = off + lax.broadcasted_iota(jnp.int32, (tk, tq), 0)
            qpos = i * tq + lax.broadcasted_iota(jnp.int32, (tk, tq), 1)
            s = jnp.where(n == cur1, jnp.where(kpos <= qpos, s, NEG), picked)
        else:
            s = picked
        return _softmax_step(s, vt_ref[:, pl.ds(off, tk)], *carry)

    carry = lax.fori_loop(0, i * per, lambda n, c: tile(n, c, False), init)
    carry = lax.fori_loop(i * per, (i + 1) * per, lambda n, c: tile(n, c, True), carry)
    _, l, acc = carry
    o_ref[...] = acc / l


def _moba_prompt(mqt, mkb, mvt, means, tq):
    t = mkb.shape[0]
    nb = means.shape[0]
    g = MOBA_GROUP
    return pl.pallas_call(
        functools.partial(_moba_prompt_kernel, tq),
        grid=(MOBA_HEADS, t // tq),
        in_specs=[pl.BlockSpec((HEAD_DIM, tq), lambda h, i: (h, i)),
                  pl.BlockSpec((t, LANE), lambda h, i: (0, h // (2 * g))),
                  pl.BlockSpec((HEAD_DIM, t), lambda h, i: (h // g, 0)),
                  pl.BlockSpec((nb, LANE), lambda h, i: (0, h // (2 * g)))],
        out_specs=pl.BlockSpec((HEAD_DIM, tq), lambda h, i: (h, i)),
        out_shape=jax.ShapeDtypeStruct((MOBA_HEADS * HEAD_DIM, t), F32),
        scratch_shapes=[pltpu.VMEM((nb, tq), F32)],
        compiler_params=pltpu.CompilerParams(dimension_semantics=("parallel", "arbitrary"),
                                             vmem_limit_bytes=VMEM_LIMIT),
        name="moba_prompt",
    )(mqt, mkb, mvt, means)


def _merge_kernel(transposed, x_ref, oa_ref, ob_ref, gattn, wgate, woa, wob, wout, gffn, wr_hi, wr_lo, br,
                  x1_o, hn_o, gates_o):
    x = x_ref[...]
    d = x.shape[1]
    h = (_rms(x) * gattn[...]).astype(BF16)
    gate = jax.nn.sigmoid(_dot(h, wgate[...]))
    oa = oa_ref[...]
    ob = ob_ref[...]
    if transposed:
        oa = oa.T
        ob = ob.T
    mix = gate[:, :d] * _dot(oa.astype(BF16), woa[...]) + gate[:, d:] * _dot(ob.astype(BF16), wob[...])
    x1 = x + _dot(mix.astype(BF16), wout[...])
    x1_o[...] = x1
    hn = _rms(x1) * gffn[...]
    hn_o[...] = hn.astype(BF16)

    hhi, hlo = _split_bf16(hn)
    lg = _dot(hhi, wr_hi[...]) + _dot(hlo, wr_hi[...]) + _dot(hhi, wr_lo[...]) + br[...]
    lane = lax.broadcasted_iota(jnp.int32, lg.shape, 1)
    is_g = (lane >= N_EXPERTS) & (lane < N_EXPERTS + N_GROUPS)
    gl = jnp.where(is_g, lg, NEG)
    gmax = jnp.max(gl, axis=-1, keepdims=True)
    gsum = jnp.sum(jnp.where(is_g, jnp.exp(gl - gmax), 0.0), axis=-1, keepdims=True)
    p_g = 1.0 / gsum
    g_idx = jnp.min(jnp.where(gl == gmax, lane - N_EXPERTS, LANE), axis=-1, keepdims=True)
    in_grp = (lane < N_EXPERTS) & (lane // EXPERTS_PER_GROUP == g_idx)
    el = jnp.where(in_grp, lg, NEG)
    v1 = jnp.max(el, axis=-1, keepdims=True)
    i1 = jnp.min(jnp.where(el == v1, lane, LANE), axis=-1, keepdims=True)
    el2 = jnp.where(lane == i1, NEG, el)
    v2 = jnp.max(el2, axis=-1, keepdims=True)
    i2 = jnp.min(jnp.where(el2 == v2, lane, LANE), axis=-1, keepdims=True)
    e2 = jnp.exp(v2 - v1)
    w1 = p_g / (1.0 + e2)
    w2 = p_g * e2 / (1.0 + e2)
    gates_o[...] = jnp.where(lane == i1, w1, 0.0) + jnp.where(lane == i2, w2, 0.0)


def _merge(x2d, oa, ob, w, transposed, tm):
    t, d = x2d.shape
    da = MLA_HEADS * MLA_V
    db = MOBA_HEADS * HEAD_DIM
    wargs = [w["gattn"], w["wgate"], w["woa"], w["wob_p" if transposed else "wob_s"], w["wout"], w["gffn"],
             w["wr_hi"], w["wr_lo"], w["br"]]

    def full(a):
        return pl.BlockSpec(a.shape, lambda i: (0,) * a.ndim)

    if transposed:
        o_specs = [pl.BlockSpec((da, tm), lambda i: (0, i)), pl.BlockSpec((db, tm), lambda i: (0, i))]
    else:
        o_specs = [pl.BlockSpec((tm, da), lambda i: (i, 0)), pl.BlockSpec((tm, db), lambda i: (i, 0))]
    return pl.pallas_call(
        functools.partial(_merge_kernel, transposed),
        grid=(t // tm,),
        in_specs=[pl.BlockSpec((tm, d), lambda i: (i, 0))] + o_specs + [full(a) for a in wargs],
        out_specs=[pl.BlockSpec((tm, d), lambda i: (i, 0)), pl.BlockSpec((tm, d), lambda i: (i, 0)),
                   pl.BlockSpec((tm, LANE), lambda i: (i, 0))],
        out_shape=[jax.ShapeDtypeStruct((t, d), F32), jax.ShapeDtypeStruct((t, d), BF16),
                   jax.ShapeDtypeStruct((t, LANE), F32)],
        compiler_params=pltpu.CompilerParams(dimension_semantics=("parallel",), vmem_limit_bytes=VMEM_LIMIT),
        name="merge_prompt" if transposed else "merge_sample",
    )(x2d, oa, ob, *wargs)


def _moe_kernel(x1_ref, hn_ref, gates_ref, wg_ref, wu_ref, wd_ref, o_ref):
    e = pl.program_id(1)

    @pl.when(e == 0)
    def _():
        o_ref[...] = x1_ref[...]

    gates = gates_ref[...]
    lane = lax.broadcasted_iota(jnp.int32, gates.shape, 1)
    gcol = jnp.sum(jnp.where(lane == e, gates, 0.0), axis=-1, keepdims=True)
    hn = hn_ref[...]
    a = _dot(hn, wg_ref[0])
    hid = a * jax.nn.sigmoid(a) * _dot(hn, wu_ref[0])
    o_ref[...] += _dot((hid * gcol).astype(BF16), wd_ref[0])


def _moe(x1, hn, gates, w, tm):
    t, d = x1.shape
    ne, _, f = w["wg"].shape
    return pl.pallas_call(
        _moe_kernel,
        grid=(t // tm, ne),
        in_specs=[pl.BlockSpec((tm, d), lambda i, e: (i, 0)),
                  pl.BlockSpec((tm, d), lambda i, e: (i, 0)),
                  pl.BlockSpec((tm, LANE), lambda i, e: (i, 0)),
                  pl.BlockSpec((1, d, f), lambda i, e: (e, 0, 0)),
                  pl.BlockSpec((1, d, f), lambda i, e: (e, 0, 0)),
                  pl.BlockSpec((1, f, d), lambda i, e: (e, 0, 0))],
        out_specs=pl.BlockSpec((tm, d), lambda i, e: (i, 0)),
        out_shape=jax.ShapeDtypeStruct((t, d), F32),
        compiler_params=pltpu.CompilerParams(dimension_semantics=("parallel", "arbitrary"),
                                             vmem_limit_bytes=VMEM_LIMIT),
        name="moe_experts",
    )(x1, hn, gates, w["wg"], w["wu"], w["wd"])


def _page_copies(pt_ref, b, chunk, slot, pages_per_chunk, srcs, bufs, sems):
    out = []
    for j in range(pages_per_chunk):
        pg = pt_ref[b, chunk * pages_per_chunk + j]
        for a, (src, buf) in enumerate(zip(srcs, bufs)):
            out.append(pltpu.make_async_copy(src.at[pg], buf.at[slot, j], sems.at[a, slot]))
    return out


def _lane_softmax_step(s, v, m, l, acc):
    m_new = jnp.maximum(m, jnp.max(s, axis=-1, keepdims=True))
    alpha = jnp.exp2(m - m_new)
    p = jnp.exp2(s - m_new)
    l = alpha * l + jnp.sum(p, axis=-1, keepdims=True)
    acc = alpha * acc + _dot(p.astype(BF16), v)
    return m_new, l, acc


def _mla_sample_kernel(cp, pt_ref, q_ref, cnew_ref, rnew_ref, cckv, ckr, wukt, gkn, wuvp, o_ref,
                       ckv_buf, kr_buf, lhs, sems):
    b = pl.program_id(0)
    npages = pt_ref.shape[1]
    nchunk = npages // cp
    ts = q_ref.shape[1]
    nh = MLA_HEADS
    nrow = nh * ts
    page = ckv_buf.shape[2]
    tk = cp * page
    kvl = ckv_buf.shape[3]
    hk = nh * MLA_NOPE

    def copies(c, slot):
        return _page_copies(pt_ref, b, c, slot, cp, (cckv, ckr), (ckv_buf, kr_buf), sems)

    for cpy in copies(0, 0):
        cpy.start()

    q = q_ref[0]
    qg = (q * gkn[...]).astype(BF16)
    qroll = pltpu.roll(q, MLA_HEADS * QPAD - MLA_NOPE, axis=1)
    lhs[0:hk, :] = wukt[...]
    qr_rows = []
    for h in range(nh):
        qh = qg[:, h * QPAD:h * QPAD + MLA_NOPE]
        lhs[hk + h * ts:hk + (h + 1) * ts, :] = _dot(qh, wukt[h * MLA_NOPE:(h + 1) * MLA_NOPE, :]).astype(BF16)
        qr_rows.append(qroll[:, h * QPAD:h * QPAD + MLA_ROPE])
    qr = jnp.concatenate(qr_rows, axis=0).astype(BF16)

    def scores(ckv_c, kr_c):
        n = ckv_c.shape[0]
        r_all = _dot_nt(lhs[...], ckv_c)
        kn = r_all[:hk].reshape(nh, MLA_NOPE, n)
        rinv = lax.rsqrt(jnp.sum(kn * kn, axis=1) * (1.0 / MLA_NOPE) + NORM_EPS)
        s_abs = r_all[hk:].reshape(nh, ts, n) * rinv[:, None, :]
        return s_abs.reshape(nrow, n) + _dot_nt(qr, kr_c)

    def body(c, carry):
        slot = c % 2

        @pl.when(c + 1 < nchunk)
        def _():
            for cpy in copies(c + 1, 1 - slot):
                cpy.start()

        for cpy in copies(c, slot):
            cpy.wait()
        ckv_c = ckv_buf[slot].reshape(tk, kvl).astype(BF16)
        kr_c = kr_buf[slot].reshape(tk, MLA_ROPE).astype(BF16)
        return _lane_softmax_step(scores(ckv_c, kr_c), ckv_c, *carry)

    init = (jnp.full((nrow, 1), NEG, F32), jnp.zeros((nrow, 1), F32), jnp.zeros((nrow, kvl), F32))
    carry = lax.fori_loop(0, nchunk, body, init)

    pad = LANE - ts
    cnew = jnp.concatenate([cnew_ref[0], jnp.zeros((pad, kvl), F32)], axis=0).astype(BF16)
    rnew = jnp.concatenate([rnew_ref[0], jnp.zeros((pad, MLA_ROPE), F32)], axis=0).astype(BF16)
    s = scores(cnew, rnew)
    key = lax.broadcasted_iota(jnp.int32, s.shape, 1)
    tok = lax.broadcasted_iota(jnp.int32, s.shape, 0) % ts
    s = jnp.where(key <= tok, s, NEG)
    _, l, acc = _lane_softmax_step(s, cnew, *carry)
    lat = (acc / l).astype(BF16)
    out = jnp.zeros((ts, nh * MLA_V), F32)
    for h in range(nh):
        out = out + _dot(lat[h * ts:(h + 1) * ts, :], wuvp[h])
    o_ref[0] = out


def _mla_sample(page_table, q_s, ckv_new, kr_new, cache_ckv, cache_krope, w):
    bsz, npages = page_table.shape
    ts = q_s.shape[1]
    page, kvl = cache_ckv.shape[1:]
    cp = math.gcd(npages, 8)
    nh = MLA_HEADS
    grid_spec = pltpu.PrefetchScalarGridSpec(
        num_scalar_prefetch=1,
        grid=(bsz,),
        in_specs=[pl.BlockSpec((1, ts, nh * QPAD), lambda b, pt: (b, 0, 0)),
                  pl.BlockSpec((1, ts, kvl), lambda b, pt: (b, 0, 0)),
                  pl.BlockSpec((1, ts, MLA_ROPE), lambda b, pt: (b, 0, 0)),
                  pl.BlockSpec(memory_space=pl.ANY),
                  pl.BlockSpec(memory_space=pl.ANY),
                  pl.BlockSpec(w["wukt"].shape, lambda b, pt: (0, 0)),
                  pl.BlockSpec(w["gkn"].shape, lambda b, pt: (0, 0)),
                  pl.BlockSpec(w["wuv_pad"].shape, lambda b, pt: (0, 0, 0))],
        out_specs=pl.BlockSpec((1, ts, nh * MLA_V), lambda b, pt: (b, 0, 0)),
        scratch_shapes=[pltpu.VMEM((2, cp, page, kvl), F32),
                        pltpu.VMEM((2, cp, page, MLA_ROPE), F32),
                        pltpu.VMEM((nh * MLA_NOPE + nh * ts, kvl), BF16),
                        pltpu.SemaphoreType.DMA((2, 2))],
    )
    return pl.pallas_call(
        functools.partial(_mla_sample_kernel, cp),
        grid_spec=grid_spec,
        out_shape=jax.ShapeDtypeStruct((bsz, ts, nh * MLA_V), F32),
        compiler_params=pltpu.CompilerParams(dimension_semantics=("arbitrary",), vmem_limit_bytes=VMEM_LIMIT),
        name="mla_sample",
    )(page_table, q_s, ckv_new, kr_new, cache_ckv, cache_krope, w["wukt"], w["gkn"], w["wuv_pad"])


def _moba_sample_kernel(cp, pt_ref, q_ref, knew_ref, vnew_ref, ck, cv, o_ref,
                        kbuf, vbuf, kbf, means, sems):
    b = pl.program_id(0)
    npages = pt_ref.shape[1]
    nchunk = npages // cp
    ts = q_ref.shape[1]
    page, wdt = kbuf.shape[2:]
    tk = cp * page
    bpc = tk // MOBA_BLOCK
    nb = means.shape[0]
    nrow = MOBA_HEADS * ts
    half = MOBA_KV_HEADS * HEAD_DIM

    def kcopies(c, slot):
        return _page_copies(pt_ref, b, c, slot, cp, (ck,), (kbuf,), sems.at[0:1])

    def vcopies(c, slot):
        return _page_copies(pt_ref, b, c, slot, cp, (cv,), (vbuf,), sems.at[1:2])

    for cpy in kcopies(0, 0):
        cpy.start()
    for cpy in vcopies(0, 0):
        cpy.start()

    mq = q_ref[0]
    lane = lax.broadcasted_iota(jnp.int32, (ts, half), 1)
    rows = []
    for kvh in range(MOBA_KV_HEADS):
        for g in range(MOBA_GROUP):
            rows.append(jnp.where(lane // HEAD_DIM == kvh, mq[:, g * half:(g + 1) * half], 0.0))
    qexp = jnp.concatenate(rows, axis=0)
    qhi, qlo = _split_bf16(qexp)

    def pass1(c, _):
        slot = c % 2

        @pl.when(c + 1 < nchunk)
        def _():
            for cpy in kcopies(c + 1, 1 - slot):
                cpy.start()

        for cpy in kcopies(c, slot):
            cpy.wait()
        kc = kbuf[slot].reshape(tk, wdt)
        means[pl.ds(pl.multiple_of(c * bpc, bpc), bpc), :] = jnp.mean(kc.reshape(bpc, MOBA_BLOCK, wdt), axis=1)
        kbf[pl.ds(pl.multiple_of(c * tk, tk), tk), :] = kc.astype(BF16)
        return 0

    lax.fori_loop(0, nchunk, pass1, 0)

    mhi, mlo = _split_bf16(means[...])
    sg = _dot_nt(qhi, mhi) + _dot_nt(qhi, mlo) + _dot_nt(qlo, mhi)
    blk = lax.broadcasted_iota(jnp.int32, sg.shape, 1)
    sel = jnp.zeros(sg.shape, F32)
    for _ in range(MOBA_TOPK):
        mx = jnp.max(sg, axis=-1, keepdims=True)
        first = jnp.min(jnp.where(sg == mx, blk, nb), axis=-1, keepdims=True)
        sel = jnp.where((blk == first) & (mx > 0.5 * NEG), 1.0, sel)
        sg = jnp.where(blk == first, NEG, sg)
    sel = sel.astype(BF16)

    def pass2(c, carry):
        slot = c % 2

        @pl.when(c + 1 < nchunk)
        def _():
            for cpy in vcopies(c + 1, 1 - slot):
                cpy.start()

        for cpy in vcopies(c, slot):
            cpy.wait()
        kc = kbf[pl.ds(pl.multiple_of(c * tk, tk), tk), :]
        s = _dot_nt(qhi, kc)
        blk_of_key = c * bpc + lax.broadcasted_iota(jnp.int32, (nb, tk), 1) // MOBA_BLOCK
        expand = (lax.broadcasted_iota(jnp.int32, (nb, tk), 0) == blk_of_key).astype(BF16)
        s = jnp.where(_dot(sel, expand) > 0.5, s, NEG)
        vc = vbuf[slot].reshape(tk, wdt).astype(BF16)
        return _lane_softmax_step(s, vc, *carry)

    init = (jnp.full((nrow, 1), NEG, F32), jnp.zeros((nrow, 1), F32), jnp.zeros((nrow, wdt), F32))
    carry = lax.fori_loop(0, nchunk, pass2, init)

    pad = LANE - ts
    knew = jnp.concatenate([knew_ref[0], jnp.zeros((pad, wdt), F32)], axis=0).astype(BF16)
    vnew = jnp.concatenate([vnew_ref[0], jnp.zeros((pad, wdt), F32)], axis=0).astype(BF16)
    s = _dot_nt(qhi, knew)
    key = lax.broadcasted_iota(jnp.int32, s.shape, 1)
    tok = lax.broadcasted_iota(jnp.int32, s.shape, 0) % ts
    s = jnp.where(key <= tok, s, NEG)
    _, l, acc = _lane_softmax_step(s, vnew, *carry)
    o = acc / l
    outs = []
    for g in range(MOBA_GROUP):
        og = jnp.zeros((ts, wdt), F32)
        for kvh in range(MOBA_KV_HEADS):
            r0 = (kvh * MOBA_GROUP + g) * ts
            og = og + jnp.where(lane // HEAD_DIM == kvh, o[r0:r0 + ts, :], 0.0)
        outs.append(og)
    o_ref[0] = jnp.concatenate(outs, axis=1)


def _moba_sample(page_table, mq_s, mk_new, mv_new, cache_k, cache_v):
    bsz, npages = page_table.shape
    ts = mq_s.shape[1]
    page, wdt = cache_k.shape[1:]
    cp = 8 * MOBA_BLOCK // page
    assert npages % cp == 0
    past = npages * page
    nb = past // MOBA_BLOCK
    grid_spec = pltpu.PrefetchScalarGridSpec(
        num_scalar_prefetch=1,
        grid=(bsz,),
        in_specs=[pl.BlockSpec((1, ts, MOBA_HEADS * HEAD_DIM), lambda b, pt: (b, 0, 0)),
                  pl.BlockSpec((1, ts, wdt), lambda b, pt: (b, 0, 0)),
                  pl.BlockSpec((1, ts, wdt), lambda b, pt: (b, 0, 0)),
                  pl.BlockSpec(memory_space=pl.ANY),
                  pl.BlockSpec(memory_space=pl.ANY)],
        out_specs=pl.BlockSpec((1, ts, MOBA_HEADS * HEAD_DIM), lambda b, pt: (b, 0, 0)),
        scratch_shapes=[pltpu.VMEM((2, cp, page, wdt), F32),
                        pltpu.VMEM((2, cp, page, wdt), F32),
                        pltpu.VMEM((past, wdt), BF16),
                        pltpu.VMEM((nb, wdt), F32),
                        pltpu.SemaphoreType.DMA((2, 2))],
    )
    return pl.pallas_call(
        functools.partial(_moba_sample_kernel, cp),
        grid_spec=grid_spec,
        out_shape=jax.ShapeDtypeStruct((bsz, ts, MOBA_HEADS * HEAD_DIM), F32),
        compiler_params=pltpu.CompilerParams(dimension_semantics=("arbitrary",), vmem_limit_bytes=VMEM_LIMIT),
        name="moba_sample",
    )(page_table, mq_s, mk_new, mv_new, cache_k, cache_v)


def _prep_weights(g_attn, w_in, g_q_lora, g_kv_lora, w_uq, g_q_nope, g_q_rope, g_k_rope, w_uk, w_uv, g_k_nope,
                  g_moba_q, g_moba_k, w_oa, w_ob, w_out, g_ffn, w_group, b_group, w_expert_router,
                  b_expert_router, w_exp_gate, w_exp_up, w_exp_down):
    d = w_in.shape[0]
    ql = g_q_lora.shape[0]
    kvl = g_kv_lora.shape[0]
    nq = MOBA_HEADS * HEAD_DIM
    nkv = MOBA_KV_HEADS * HEAD_DIM
    sizes = (ql, kvl, MLA_ROPE, nq, nkv, nkv, d, d)
    assert sum(sizes) == w_in.shape[1]
    parts, o = [], 0
    for s in sizes:
        parts.append(w_in[:, o:o + s])
        o += s
    cq, ckv, kr, mq, mk, mv, ga, gb = parts
    sm = HEAD_DIM ** -0.5 * LOG2E
    sq = (MLA_NOPE + MLA_ROPE) ** -0.5 * LOG2E
    row = lambda v: v.reshape(1, -1).astype(F32)

    def w_in_layout(mqc):
        return jnp.concatenate(
            [cq, ckv, mqc, _swap_halves(mqc, HEAD_DIM), mk, _swap_halves(mk, HEAD_DIM), mv,
             _pad_lanes(kr, LANE), _pad_lanes(_swap_halves(kr, MLA_ROPE), LANE)], axis=1).astype(BF16)

    mq_gkd = mq.reshape(d, MOBA_KV_HEADS, MOBA_GROUP, HEAD_DIM).transpose(0, 2, 1, 3).reshape(d, nq)
    w = {"g_q_lora": g_q_lora, "g_kv_lora": g_kv_lora}
    w["w_in_p"] = w_in_layout(mq)
    w["w_in_s"] = w_in_layout(mq_gkd)
    w["gattn"] = row(g_attn)
    w["gql"] = row(g_q_lora)
    w["gkvl"] = row(g_kv_lora)
    w["gmq"] = row(jnp.tile(g_moba_q, MOBA_HEADS)) * sm
    w["gmqs"] = row(jnp.tile(_swap_halves(g_moba_q, HEAD_DIM), MOBA_HEADS)) * sm
    w["gmk"] = row(jnp.tile(g_moba_k, MOBA_KV_HEADS))
    w["gmks"] = row(jnp.tile(_swap_halves(g_moba_k, HEAD_DIM), MOBA_KV_HEADS))
    w["gkr"] = row(_pad_lanes(g_k_rope, LANE))
    w["gkrs"] = row(_pad_lanes(_swap_halves(g_k_rope, MLA_ROPE), LANE))
    w["g512"] = _group_matrix(nq, [(0, HEAD_DIM)], HEAD_DIM)
    w["g128"] = _group_matrix(LANE, [(0, MLA_ROPE)], LANE)
    w["g1024"] = _group_matrix(MLA_HEADS * QPAD, [(0, MLA_NOPE), (MLA_NOPE, MLA_ROPE)], QPAD)
    tail = QPAD - MLA_NOPE - MLA_ROPE
    w["icq"] = row(jnp.tile(jnp.concatenate([jnp.full((MLA_NOPE,), 1.0 / MLA_NOPE), jnp.full((MLA_ROPE,), 1.0 / MLA_ROPE),
                                             jnp.ones((tail,))]), MLA_HEADS))
    uq = w_uq.reshape(ql, MLA_HEADS, MLA_NOPE + MLA_ROPE)
    nope, rope = uq[..., :MLA_NOPE], uq[..., MLA_NOPE:]
    zt = jnp.zeros((ql, MLA_HEADS, tail), F32)
    w["wuqa"] = jnp.concatenate([nope, rope, zt], axis=-1).reshape(ql, -1).astype(BF16)
    w["wuqb"] = jnp.concatenate([jnp.zeros_like(nope), _swap_halves(rope, MLA_ROPE), zt], axis=-1).reshape(ql, -1).astype(BF16)
    z1 = jnp.zeros((tail,), F32)
    w["gqa"] = row(jnp.tile(jnp.concatenate([g_q_nope, g_q_rope, z1]), MLA_HEADS)) * sq
    w["gqb"] = row(jnp.tile(jnp.concatenate([jnp.zeros((MLA_NOPE,), F32), _swap_halves(g_q_rope, MLA_ROPE), z1]),
                            MLA_HEADS)) * sq
    w["wuk_pad"] = _pad_lanes(w_uk, QPAD).reshape(kvl, -1).astype(BF16)
    w["gkn"] = row(jnp.tile(_pad_lanes(g_k_nope, QPAD), MLA_HEADS))
    j = jnp.arange(LANE)[:, None]
    c = jnp.arange(MLA_HEADS * QPAD)[None, :]
    w["ekr"] = ((j < MLA_ROPE) & (c % QPAD == MLA_NOPE + j)).astype(BF16)
    w["wuv"] = w_uv.reshape(kvl, -1).astype(BF16)
    w["wukt"] = w_uk.reshape(kvl, -1).T.astype(BF16)
    head_of_col = jnp.arange(MLA_HEADS * MLA_V) // MLA_V
    w["wuv_pad"] = jnp.where(head_of_col[None, None, :] == jnp.arange(MLA_HEADS)[:, None, None],
                             w_uv.reshape(1, kvl, -1), 0.0).astype(BF16)

    w["wgate"] = jnp.concatenate([ga, gb], axis=1).astype(BF16)
    w["woa"] = w_oa.astype(BF16)
    w["wob_p"] = w_ob.astype(BF16)
    w["wob_s"] = w_ob.reshape(MOBA_KV_HEADS, MOBA_GROUP, HEAD_DIM, -1).transpose(1, 0, 2, 3).reshape(nq, -1).astype(BF16)
    w["wout"] = w_out.astype(BF16)
    w["gffn"] = row(g_ffn)
    wr = _pad_lanes(jnp.concatenate([w_expert_router, w_group], axis=1), LANE)
    w["wr_hi"], w["wr_lo"] = _split_bf16(wr)
    w["br"] = row(_pad_lanes(jnp.concatenate([b_expert_router, b_group]), LANE))
    w["wg"] = w_exp_gate.astype(BF16)
    w["wu"] = w_exp_up.astype(BF16)
    w["wd"] = w_exp_down.astype(BF16)
    return w


def _row_tile(t, want):
    return want if t % want == 0 else t


def kernel(x_prompt, x_sample, cache_ckv, cache_krope, cache_k, cache_v, page_table, g_attn, w_in, g_q_lora, g_kv_lora, w_uq, g_q_nope, g_q_rope, g_k_rope, w_uk, w_uv, g_k_nope, g_moba_q, g_moba_k, w_oa, w_ob, w_out, g_ffn, w_group, b_group, w_expert_router, b_expert_router, w_exp_gate, w_exp_up, w_exp_down):
    w = _prep_weights(g_attn, w_in, g_q_lora, g_kv_lora, w_uq, g_q_nope, g_q_rope, g_k_rope, w_uk, w_uv, g_k_nope,
                      g_moba_q, g_moba_k, w_oa, w_ob, w_out, g_ffn, w_group, b_group, w_expert_router,
                      b_expert_router, w_exp_gate, w_exp_up, w_exp_down)
    bp, t, d = x_prompt.shape
    bs, ts, _ = x_sample.shape
    npool, page = cache_ckv.shape[:2]
    past = page_table.shape[1] * page
    assert bp == 1 and t % MOBA_BLOCK == 0
    assert past % MOBA_BLOCK == 0 and ts <= LANE and ts % 8 == 0

    xp = x_prompt.reshape(t, d)
    ckv_p, kr_p, mk_p, mv_p, qt, kp, vt, mqt, mkb, mvt = _features(
        xp, jnp.arange(t, dtype=jnp.int32), w, True, _row_tile(t, 256))
    tq = _row_tile(t, 512)
    oat = _mla_prompt(qt, kp, vt, tq, tq)
    obt = _moba_prompt(mqt, mkb, mvt, _block_means(mk_p), tq)
    x1, hn, gates = _merge(xp, oat, obt, w, True, _row_tile(t, 256))
    y_p = _moe(x1, hn, gates, w, _row_tile(t, 1024))

    xs = x_sample.reshape(bs * ts, d)
    pos_s = past + jnp.tile(jnp.arange(ts, dtype=jnp.int32), bs)
    ckv_s, kr_s, mk_s, mv_s, q_s, mq_s = _features(xs, pos_s, w, False, _row_tile(bs * ts, 256))
    kvl = ckv_s.shape[1]
    nkv = MOBA_KV_HEADS * HEAD_DIM
    oa_s = _mla_sample(page_table, q_s.reshape(bs, ts, -1), ckv_s.reshape(bs, ts, kvl), kr_s.reshape(bs, ts, -1),
                       cache_ckv, cache_krope, w)
    ob_s = _moba_sample(page_table, mq_s.reshape(bs, ts, -1), mk_s.reshape(bs, ts, nkv), mv_s.reshape(bs, ts, nkv),
                        cache_k.reshape(npool, page, nkv), cache_v.reshape(npool, page, nkv))
    x1s, hns, gates_s = _merge(xs, oa_s.reshape(bs * ts, -1), ob_s.reshape(bs * ts, -1), w, False,
                               _row_tile(bs * ts, 256))
    y_s = _moe(x1s, hns, gates_s, w, _row_tile(bs * ts, 1024))

    return (y_p.reshape(bp, t, d), y_s.reshape(bs, ts, d),
            ckv_p.reshape(bp, t, kvl), kr_p.reshape(bp, t, MLA_ROPE),
            mk_p.reshape(bp, t, MOBA_KV_HEADS, HEAD_DIM), mv_p.reshape(bp, t, MOBA_KV_HEADS, HEAD_DIM),
            ckv_s.reshape(bs, ts, kvl), kr_s.reshape(bs, ts, MLA_ROPE),
            mk_s.reshape(bs, ts, MOBA_KV_HEADS, HEAD_DIM), mv_s.reshape(bs, ts, MOBA_KV_HEADS, HEAD_DIM))
```

```python
import functools
import math

import jax
import jax.numpy as jnp
from jax import lax
from jax.experimental import pallas as pl
from jax.experimental.pallas import tpu as pltpu

MLA_HEADS = 8
MLA_NOPE = 64
MLA_ROPE = 32
MLA_V = 64
MOBA_HEADS = 8
MOBA_KV_HEADS = 4
MOBA_GROUP = MOBA_HEADS // MOBA_KV_HEADS
HEAD_DIM = 64
MOBA_BLOCK = 256
MOBA_TOPK = 3
N_GROUPS = 4
EXPERTS_PER_GROUP = 8
N_EXPERTS = N_GROUPS * EXPERTS_PER_GROUP
ROPE_THETA = 10000.0
NORM_EPS = 1e-6

LANE = 128
QPAD = LANE
LOG2E = 1.4426950408889634
NEG = -1e30
EXP2_SAFE = 40.0
BF16_SLACK = 1.02
ATTN_TILE_GROUP = 2
MLA_SAMPLE_PAGES_PER_CHUNK = 16
MOBA_SAMPLE_PAGES_PER_CHUNK = 8
VMEM_LIMIT = 56 * 1024 * 1024

F32 = jnp.float32
BF16 = jnp.bfloat16


def _dot(a, b):
    return jnp.dot(a, b, preferred_element_type=F32)


def _dot_nt(a, b):
    return lax.dot_general(a, b, (((1,), (1,)), ((), ())), preferred_element_type=F32)


def _split_bf16(x):
    hi = x.astype(BF16)
    lo = (x - hi.astype(F32)).astype(BF16)
    return hi, lo


def _group_ssq(x, g):
    hi, lo = _split_bf16(x * x)
    return _dot(hi, g) + _dot(lo, g)


def _rms(x):
    return x * lax.rsqrt(jnp.mean(x * x, axis=-1, keepdims=True) + NORM_EPS)


def _tile_lanes(x, n):
    return x if n == 1 else jnp.concatenate([x] * n, axis=1)


def _feat_kernel(prompt, dims, *refs):
    ql, kvl = dims
    (x_ref, gattn, w_in, gql, gkvl, gmq, gmqs, gmk, gmks, gkr, gkrs,
     c64, s64, c32, s32, cqt, sqt, g512, g128, g1024, icq, wuqa, wuqb, gqa, gqb) = refs[:25]
    rest = refs[25:]
    x = x_ref[...]
    h = (_rms(x) * gattn[...]).astype(BF16)
    z = _dot(h, w_in[...])
    o = 0
    cq = z[:, o:o + ql]; o += ql
    ckv = z[:, o:o + kvl]; o += kvl
    mq = z[:, o:o + 512]; o += 512
    mqs = z[:, o:o + 512]; o += 512
    mk = z[:, o:o + 256]; o += 256
    mks = z[:, o:o + 256]; o += 256
    mv = z[:, o:o + 256]; o += 256
    kr = z[:, o:o + 128]; o += 128
    krs = z[:, o:o + 128]; o += 128

    cqn = _rms(cq) * gql[...]
    ckvn = _rms(ckv) * gkvl[...]

    r = lax.rsqrt(_group_ssq(kr, g128[...]) * (1.0 / MLA_ROPE) + NORM_EPS)
    krn = r * (kr * gkr[...] * c32[...] + krs * gkrs[...] * s32[...])

    c2 = _tile_lanes(c64[...], 2)
    s2 = _tile_lanes(s64[...], 2)
    r = lax.rsqrt(_group_ssq(mk, g512[:256, :256]) * (1.0 / HEAD_DIM) + NORM_EPS)
    mkn = r * (mk * gmk[...] * c2 + mks * gmks[...] * s2)

    c4 = _tile_lanes(c2, 2)
    s4 = _tile_lanes(s2, 2)
    r = lax.rsqrt(_group_ssq(mq, g512[...]) * (1.0 / HEAD_DIM) + NORM_EPS)
    mqn = r * (mq * gmq[...] * c4 + mqs * gmqs[...] * s4)

    cqb = cqn.astype(BF16)
    qa = _dot(cqb, wuqa[...])
    qb = _dot(cqb, wuqb[...])
    r = lax.rsqrt(_group_ssq(qa, g1024[...]) * icq[...] + NORM_EPS)
    c8 = _tile_lanes(cqt[...], MLA_HEADS)
    s8 = _tile_lanes(sqt[...], MLA_HEADS)
    qn = r * (qa * gqa[...] * c8 + qb * gqb[...] * s8)

    if prompt:
        wuk, gkn, ekr, wuv = rest[:4]
        ckv_o, kr_o, mk_o, mv_o, qt_o, kp_o, vt_o, mqt_o, mkb_o, mvt_o = rest[4:]
    else:
        ckv_o, kr_o, mk_o, mv_o, q_o, mq_o = rest
    ckv_o[...] = ckvn
    kr_o[...] = krn[:, :MLA_ROPE]
    mk_o[...] = mkn
    mv_o[...] = mv
    if prompt:
        ckvb = ckvn.astype(BF16)
        kraw = _dot(ckvb, wuk[...])
        rk = lax.rsqrt(_group_ssq(kraw, g1024[...]) * icq[...] + NORM_EPS)
        kp = rk * kraw * gkn[...] + _dot(krn.astype(BF16), ekr[...])
        kp_o[...] = kp.astype(BF16)
        qt_o[...] = qn.T.astype(BF16)
        vt_o[...] = _dot(ckvb, wuv[...]).T.astype(BF16)
        mqt_o[...] = mqn.T.astype(BF16)
        mkb_o[...] = mkn.astype(BF16)
        mvt_o[...] = mv.T.astype(BF16)
    else:
        q_o[...] = qn
        mq_o[...] = mqn


def _swap_halves(w, hd):
    lead = w.shape[:-1]
    n = w.shape[-1] // hd
    w = w.reshape(lead + (n, 2, hd // 2))
    return jnp.flip(w, axis=-2).reshape(lead + (n * hd,))


def _pad_lanes(w, width):
    return jnp.pad(w, [(0, 0)] * (w.ndim - 1) + [(0, width - w.shape[-1])])


def _group_matrix(width, starts_sizes, period):
    lane = jnp.arange(width)
    blk = lane // period
    off = lane % period
    gid = jnp.full((width,), -1, jnp.int32)
    for k, (st, sz) in enumerate(starts_sizes):
        gid = jnp.where((off >= st) & (off < st + sz), k, gid)
    same = (blk[:, None] == blk[None, :]) & (gid[:, None] == gid[None, :]) & (gid[:, None] >= 0)
    return same.astype(BF16)


def _rope_tables(pos, half):
    inv = ROPE_THETA ** (-jnp.arange(half, dtype=F32) / half)
    ang = pos.astype(F32)[:, None] * inv[None, :]
    return jnp.cos(ang), jnp.sin(ang)


def _features(x2d, pos, w, prompt, tm):
    t, d = x2d.shape
    ql = w["g_q_lora"].shape[0]
    kvl = w["g_kv_lora"].shape[0]
    assert t % tm == 0
    cos32, sin32 = _rope_tables(pos, HEAD_DIM // 2)
    cos16, sin16 = _rope_tables(pos, MLA_ROPE // 2)
    c64 = jnp.concatenate([cos32, cos32] * 2, axis=1)
    s64 = jnp.concatenate([-sin32, sin32] * 2, axis=1)
    z96 = jnp.zeros((t, LANE - MLA_ROPE), F32)
    c32 = jnp.concatenate([cos16, cos16, z96], axis=1)
    s32 = jnp.concatenate([-sin16, sin16, z96], axis=1)
    z32 = jnp.zeros((t, QPAD - MLA_NOPE - MLA_ROPE), F32)
    cqt = jnp.concatenate([jnp.ones((t, MLA_NOPE), F32), cos16, cos16, z32], axis=1)
    sqt = jnp.concatenate([jnp.zeros((t, MLA_NOPE), F32), -sin16, sin16, z32], axis=1)

    row_args = [x2d, w["gattn"], w["w_in_p" if prompt else "w_in_s"], w["gql"], w["gkvl"],
                w["gmq"], w["gmqs"], w["gmk"], w["gmks"], w["gkr"], w["gkrs"]]
    tab_args = [c64, s64, c32, s32, cqt, sqt]
    mat_args = [w["g512"], w["g128"], w["g1024"], w["icq"], w["wuqa"], w["wuqb"], w["gqa"], w["gqb"]]
    if prompt:
        mat_args += [w["wuk_pad"], w["gkn"], w["ekr"], w["wuv"]]

    def full(a):
        return pl.BlockSpec(a.shape, lambda i: (0,) * a.ndim)

    in_specs = [pl.BlockSpec((tm, d), lambda i: (i, 0))] + [full(a) for a in row_args[1:]]
    in_specs += [pl.BlockSpec((tm, LANE), lambda i: (i, 0)) for _ in tab_args]
    in_specs += [full(a) for a in mat_args]

    def rows(width, dtype):
        return jax.ShapeDtypeStruct((t, width), dtype), pl.BlockSpec((tm, width), lambda i: (i, 0))

    def cols(height, dtype):
        return jax.ShapeDtypeStruct((height, t), dtype), pl.BlockSpec((height, tm), lambda i: (0, i))

    outs = [rows(kvl, F32), rows(MLA_ROPE, F32), rows(256, F32), rows(256, F32)]
    if prompt:
        outs += [cols(MLA_HEADS * QPAD, BF16), rows(MLA_HEADS * QPAD, BF16), cols(MLA_HEADS * MLA_V, BF16),
                 cols(512, BF16), rows(256, BF16), cols(256, BF16)]
    else:
        outs += [rows(MLA_HEADS * QPAD, F32), rows(512, F32)]
    return pl.pallas_call(
        functools.partial(_feat_kernel, prompt, (ql, kvl)),
        grid=(t // tm,),
        in_specs=in_specs,
        out_specs=[o[1] for o in outs],
        out_shape=[o[0] for o in outs],
        compiler_params=pltpu.CompilerParams(dimension_semantics=("parallel",), vmem_limit_bytes=VMEM_LIMIT),
        name="features_prompt" if prompt else "features_sample",
    )(*row_args, *tab_args, *mat_args)


def _softmax_init(online, dv, tq):
    if online:
        return (jnp.full((1, tq), NEG, F32), jnp.zeros((1, tq), F32), jnp.zeros((dv, tq), F32))
    return (jnp.zeros((8, tq), F32), jnp.zeros((dv, tq), F32))


def _softmax_step(online, s, vt, carry):
    if online:
        m, l, acc = carry
        m_new = jnp.maximum(m, jnp.max(s, axis=0, keepdims=True))
        alpha = jnp.exp2(m - m_new)
        p = jnp.exp2(s - m_new)
        l = alpha * l + jnp.sum(p, axis=0, keepdims=True)
        acc = alpha * acc + _dot(vt, p.astype(BF16))
        return m_new, l, acc
    l8, acc = carry
    p = jnp.exp2(s)
    l8 = l8 + jnp.sum(p.reshape(s.shape[0] // 8, 8, s.shape[1]), axis=0)
    acc = acc + _dot(vt, p.astype(BF16))
    return l8, acc


def _softmax_finish(online, carry):
    if online:
        _, l, acc = carry
    else:
        l8, acc = carry
        l = jnp.sum(l8, axis=0, keepdims=True)
    return acc / l


def _key_tile_loops(i, tq, tk, group, tile, carry):
    big = group * tk
    n_big = (i * tq) // big
    carry = lax.fori_loop(0, n_big, lambda j, c: tile(pl.multiple_of(j * big, big), big, c, False), carry)
    carry = lax.fori_loop(n_big * group, (i * tq) // tk,
                          lambda j, c: tile(pl.multiple_of(j * tk, tk), tk, c, False), carry)
    for u in range(tq // tk):
        carry = tile(pl.multiple_of(i * tq + u * tk, tk), tk, carry, True)
    return carry


def _mla_prompt_kernel(online, tq, tk, group, qt_ref, kp_ref, vt_ref, o_ref):
    i = pl.program_id(1)
    q = qt_ref[...]

    def tile(off, size, carry, masked):
        s = _dot(kp_ref[pl.ds(off, size), :], q)
        if masked:
            kpos = off + lax.broadcasted_iota(jnp.int32, (size, tq), 0)
            qpos = i * tq + lax.broadcasted_iota(jnp.int32, (size, tq), 1)
            s = jnp.where(kpos <= qpos, s, NEG)
        return _softmax_step(online, s, vt_ref[:, pl.ds(off, size)], carry)

    carry = _key_tile_loops(i, tq, tk, group, tile, _softmax_init(online, vt_ref.shape[0], tq))
    o_ref[...] = _softmax_finish(online, carry)


def _mla_prompt(online, tq, tk, group, qt, kp, vt):
    t = kp.shape[0]
    nh = MLA_HEADS
    return pl.pallas_call(
        functools.partial(_mla_prompt_kernel, online, tq, tk, group),
        grid=(nh, t // tq),
        in_specs=[pl.BlockSpec((QPAD, tq), lambda h, i: (h, i)),
                  pl.BlockSpec((t, QPAD), lambda h, i: (0, h)),
                  pl.BlockSpec((MLA_V, t), lambda h, i: (h, 0))],
        out_specs=pl.BlockSpec((MLA_V, tq), lambda h, i: (h, i)),
        out_shape=jax.ShapeDtypeStruct((nh * MLA_V, t), F32),
        compiler_params=pltpu.CompilerParams(dimension_semantics=("parallel", "arbitrary"),
                                             vmem_limit_bytes=VMEM_LIMIT),
        name="mla_prompt_online" if online else "mla_prompt",
    )(qt, kp, vt)


def _block_mean_kernel(k_ref, o_ref):
    k = k_ref[...]
    n = o_ref.shape[0]
    o_ref[...] = jnp.mean(k.reshape(n, MOBA_BLOCK, k.shape[1]), axis=1)


def _block_means(k2d):
    t, wdt = k2d.shape
    nb = t // MOBA_BLOCK
    per = 8 if nb % 8 == 0 else nb
    return pl.pallas_call(
        _block_mean_kernel,
        grid=(nb // per,),
        in_specs=[pl.BlockSpec((per * MOBA_BLOCK, wdt), lambda i: (i, 0))],
        out_specs=pl.BlockSpec((per, wdt), lambda i: (i, 0)),
        out_shape=jax.ShapeDtypeStruct((nb, wdt), F32),
        compiler_params=pltpu.CompilerParams(dimension_semantics=("parallel",)),
        name="moba_block_means",
    )(k2d)


def _select_topk_rows(s, limit):
    nb = s.shape[0]
    row = lax.broadcasted_iota(jnp.int32, s.shape, 0)
    s = jnp.where(row < limit, s, NEG)
    sel = jnp.zeros(s.shape, F32)
    for _ in range(MOBA_TOPK):
        mx = jnp.max(s, axis=0, keepdims=True)
        first = jnp.min(jnp.where(s == mx, row, nb), axis=0, keepdims=True)
        hit = (row == first) & (mx > 0.5 * NEG)
        sel = jnp.where(hit, 1.0, sel)
        s = jnp.where(row == first, NEG, s)
    return sel


def _moba_prompt_kernel(online, tq, group, q_ref, k_ref, vt_ref, mean_ref, o_ref, sel_ref):
    h = pl.program_id(0)
    i = pl.program_id(1)
    tk = MOBA_BLOCK
    odd = (h // MOBA_GROUP) % 2
    q64 = q_ref[...]
    zero = jnp.zeros_like(q64)
    q = jnp.concatenate([jnp.where(odd == 0, q64, zero), jnp.where(odd == 1, q64, zero)], axis=0)
    qpos1 = i * tq + lax.broadcasted_iota(jnp.int32, (1, tq), 1)
    cur1 = qpos1 // MOBA_BLOCK
    mhi, mlo = _split_bf16(mean_ref[...])
    sel_ref[...] = _select_topk_rows(_dot(mhi, q) + _dot(mlo, q), cur1)

    def tile(off, size, carry, own):
        s = _dot(k_ref[pl.ds(off, size), :], q)
        n0 = off // tk
        slabs = []
        for u in range(size // tk):
            su = s[u * tk:(u + 1) * tk]
            picked = jnp.where(sel_ref[pl.ds(n0 + u, 1), :] > 0.0, su, NEG)
            if own:
                kpos = off + u * tk + lax.broadcasted_iota(jnp.int32, (tk, tq), 0)
                qpos = i * tq + lax.broadcasted_iota(jnp.int32, (tk, tq), 1)
                picked = jnp.where(n0 + u == cur1, jnp.where(kpos <= qpos, su, NEG), picked)
            slabs.append(picked)
        s = slabs[0] if len(slabs) == 1 else jnp.concatenate(slabs, axis=0)
        return _softmax_step(online, s, vt_ref[:, pl.ds(off, size)], carry)

    carry = _key_tile_loops(i, tq, tq, group, tile, _softmax_init(online, vt_ref.shape[0], tq))
    o_ref[...] = _softmax_finish(online, carry)


def _moba_prompt(online, tq, group, mqt, mkb, mvt, means):
    t = mkb.shape[0]
    nb = means.shape[0]
    g = MOBA_GROUP
    return pl.pallas_call(
        functools.partial(_moba_prompt_kernel, online, tq, group),
        grid=(MOBA_HEADS, t // tq),
        in_specs=[pl.BlockSpec((HEAD_DIM, tq), lambda h, i: (h, i)),
                  pl.BlockSpec((t, LANE), lambda h, i: (0, h // (2 * g))),
                  pl.BlockSpec((HEAD_DIM, t), lambda h, i: (h // g, 0)),
                  pl.BlockSpec((nb, LANE), lambda h, i: (0, h // (2 * g)))],
        out_specs=pl.BlockSpec((HEAD_DIM, tq), lambda h, i: (h, i)),
        out_shape=jax.ShapeDtypeStruct((MOBA_HEADS * HEAD_DIM, t), F32),
        scratch_shapes=[pltpu.VMEM((nb, tq), F32)],
        compiler_params=pltpu.CompilerParams(dimension_semantics=("parallel", "arbitrary"),
                                             vmem_limit_bytes=VMEM_LIMIT),
        name="moba_prompt_online" if online else "moba_prompt",
    )(mqt, mkb, mvt, means)


def _merge_kernel(transposed, x_ref, oa_ref, ob_ref, gattn, wgate, woa, wob, wout, gffn, wr_hi, wr_lo, br,
                  x1_o, hn_o, gates_o):
    x = x_ref[...]
    d = x.shape[1]
    h = (_rms(x) * gattn[...]).astype(BF16)
    gate = jax.nn.sigmoid(_dot(h, wgate[...]))
    oa = oa_ref[...]
    ob = ob_ref[...]
    if transposed:
        oa = oa.T
        ob = ob.T
    mix = gate[:, :d] * _dot(oa.astype(BF16), woa[...]) + gate[:, d:] * _dot(ob.astype(BF16), wob[...])
    x1 = x + _dot(mix.astype(BF16), wout[...])
    x1_o[...] = x1
    hn = _rms(x1) * gffn[...]
    hn_o[...] = hn.astype(BF16)

    hhi, hlo = _split_bf16(hn)
    lg = _dot(hhi, wr_hi[...]) + _dot(hlo, wr_hi[...]) + _dot(hhi, wr_lo[...]) + br[...]
    lane = lax.broadcasted_iota(jnp.int32, lg.shape, 1)
    is_g = (lane >= N_EXPERTS) & (lane < N_EXPERTS + N_GROUPS)
    gl = jnp.where(is_g, lg, NEG)
    gmax = jnp.max(gl, axis=-1, keepdims=True)
    gsum = jnp.sum(jnp.where(is_g, jnp.exp(gl - gmax), 0.0), axis=-1, keepdims=True)
    p_g = 1.0 / gsum
    g_idx = jnp.min(jnp.where(gl == gmax, lane - N_EXPERTS, LANE), axis=-1, keepdims=True)
    in_grp = (lane < N_EXPERTS) & (lane // EXPERTS_PER_GROUP == g_idx)
    el = jnp.where(in_grp, lg, NEG)
    v1 = jnp.max(el, axis=-1, keepdims=True)
    i1 = jnp.min(jnp.where(el == v1, lane, LANE), axis=-1, keepdims=True)
    el2 = jnp.where(lane == i1, NEG, el)
    v2 = jnp.max(el2, axis=-1, keepdims=True)
    i2 = jnp.min(jnp.where(el2 == v2, lane, LANE), axis=-1, keepdims=True)
    e2 = jnp.exp(v2 - v1)
    w1 = p_g / (1.0 + e2)
    w2 = p_g * e2 / (1.0 + e2)
    gates_o[...] = jnp.where(lane == i1, w1, 0.0) + jnp.where(lane == i2, w2, 0.0)


def _merge(x2d, oa, ob, w, transposed, tm):
    t, d = x2d.shape
    da = MLA_HEADS * MLA_V
    db = MOBA_HEADS * HEAD_DIM
    wargs = [w["gattn"], w["wgate"], w["woa"], w["wob_p" if transposed else "wob_s"], w["wout"], w["gffn"],
             w["wr_hi"], w["wr_lo"], w["br"]]

    def full(a):
        return pl.BlockSpec(a.shape, lambda i: (0,) * a.ndim)

    if transposed:
        o_specs = [pl.BlockSpec((da, tm), lambda i: (0, i)), pl.BlockSpec((db, tm), lambda i: (0, i))]
    else:
        o_specs = [pl.BlockSpec((tm, da), lambda i: (i, 0)), pl.BlockSpec((tm, db), lambda i: (i, 0))]
    return pl.pallas_call(
        functools.partial(_merge_kernel, transposed),
        grid=(t // tm,),
        in_specs=[pl.BlockSpec((tm, d), lambda i: (i, 0))] + o_specs + [full(a) for a in wargs],
        out_specs=[pl.BlockSpec((tm, d), lambda i: (i, 0)), pl.BlockSpec((tm, d), lambda i: (i, 0)),
                   pl.BlockSpec((tm, LANE), lambda i: (i, 0))],
        out_shape=[jax.ShapeDtypeStruct((t, d), F32), jax.ShapeDtypeStruct((t, d), BF16),
                   jax.ShapeDtypeStruct((t, LANE), F32)],
        compiler_params=pltpu.CompilerParams(dimension_semantics=("parallel",), vmem_limit_bytes=VMEM_LIMIT),
        name="merge_prompt" if transposed else "merge_sample",
    )(x2d, oa, ob, *wargs)


def _moe_kernel(x1_ref, hn_ref, gates_ref, wg_ref, wu_ref, wd_ref, o_ref):
    e = pl.program_id(1)

    @pl.when(e == 0)
    def _():
        o_ref[...] = x1_ref[...]

    gates = gates_ref[...]
    lane = lax.broadcasted_iota(jnp.int32, gates.shape, 1)
    gcol = jnp.sum(jnp.where(lane == e, gates, 0.0), axis=-1, keepdims=True)
    hn = hn_ref[...]
    a = _dot(hn, wg_ref[0])
    hid = a * jax.nn.sigmoid(a) * _dot(hn, wu_ref[0])
    o_ref[...] += _dot((hid * gcol).astype(BF16), wd_ref[0])


def _moe(x1, hn, gates, w, tm):
    t, d = x1.shape
    ne, _, f = w["wg"].shape
    return pl.pallas_call(
        _moe_kernel,
        grid=(t // tm, ne),
        in_specs=[pl.BlockSpec((tm, d), lambda i, e: (i, 0)),
                  pl.BlockSpec((tm, d), lambda i, e: (i, 0)),
                  pl.BlockSpec((tm, LANE), lambda i, e: (i, 0)),
                  pl.BlockSpec((1, d, f), lambda i, e: (e, 0, 0)),
                  pl.BlockSpec((1, d, f), lambda i, e: (e, 0, 0)),
                  pl.BlockSpec((1, f, d), lambda i, e: (e, 0, 0))],
        out_specs=pl.BlockSpec((tm, d), lambda i, e: (i, 0)),
        out_shape=jax.ShapeDtypeStruct((t, d), F32),
        compiler_params=pltpu.CompilerParams(dimension_semantics=("parallel", "arbitrary"),
                                             vmem_limit_bytes=VMEM_LIMIT),
        name="moe_experts",
    )(x1, hn, gates, w["wg"], w["wu"], w["wd"])


def _page_copies(pt_ref, b, chunk, slot, pages_per_chunk, srcs, bufs, sems):
    out = []
    for j in range(pages_per_chunk):
        pg = pt_ref[b, chunk * pages_per_chunk + j]
        for src, buf, sem in zip(srcs, bufs, sems):
            out.append(pltpu.make_async_copy(src.at[pg], buf.at[slot, j], sem.at[slot]))
    return out


def _lane_softmax_step(s, v, m, l, acc):
    m_new = jnp.maximum(m, jnp.max(s, axis=-1, keepdims=True))
    alpha = jnp.exp2(m - m_new)
    p = jnp.exp2(s - m_new)
    l = alpha * l + jnp.sum(p, axis=-1, keepdims=True)
    acc = alpha * acc + _dot(p.astype(BF16), v)
    return m_new, l, acc


def _mla_sample_kernel(cp, pt_ref, q_ref, cnew_ref, rnew_ref, cckv, ckr, wukt, gkn, wuvp, o_ref,
                       ckv_buf, kr_buf, lhs, ckv_sem, kr_sem):
    b = pl.program_id(0)
    npages = pt_ref.shape[1]
    nchunk = npages // cp
    ts = q_ref.shape[1]
    nh = MLA_HEADS
    nrow = nh * ts
    page = ckv_buf.shape[2]
    tk = cp * page
    kvl = ckv_buf.shape[3]
    hk = nh * MLA_NOPE

    def copies(seq, c, slot):
        return _page_copies(pt_ref, seq, c, slot, cp, (cckv, ckr), (ckv_buf, kr_buf), (ckv_sem, kr_sem))

    @pl.when(b == 0)
    def _():
        for cpy in copies(0, 0, 0):
            cpy.start()

    q = q_ref[0]
    qg = (q * gkn[...]).astype(BF16)
    qroll = pltpu.roll(q, MLA_HEADS * QPAD - MLA_NOPE, axis=1)
    lhs[0:hk, :] = wukt[...]
    qr_rows = []
    for h in range(nh):
        qh = qg[:, h * QPAD:h * QPAD + MLA_NOPE]
        lhs[hk + h * ts:hk + (h + 1) * ts, :] = _dot(qh, wukt[h * MLA_NOPE:(h + 1) * MLA_NOPE, :]).astype(BF16)
        qr_rows.append(qroll[:, h * QPAD:h * QPAD + MLA_ROPE])
    qr = jnp.concatenate(qr_rows, axis=0).astype(BF16)

    def scores(ckv_c, rope_scores):
        n = ckv_c.shape[0]
        r_all = _dot_nt(lhs[...], ckv_c)
        kn = r_all[:hk].reshape(nh, MLA_NOPE, n)
        rinv = lax.rsqrt(jnp.sum(kn * kn, axis=1) * (1.0 / MLA_NOPE) + NORM_EPS)
        s_abs = r_all[hk:].reshape(nh, ts, n) * rinv[:, None, :]
        return s_abs.reshape(nrow, n) + rope_scores

    def body(c, carry):
        g = b * nchunk + c
        slot = g % 2

        @pl.when(c + 1 < nchunk)
        def _():
            for cpy in copies(b, c + 1, 1 - slot):
                cpy.start()

        @pl.when((c + 1 == nchunk) & (b + 1 < pl.num_programs(0)))
        def _():
            for cpy in copies(b + 1, 0, 1 - slot):
                cpy.start()

        for cpy in copies(b, c, slot):
            cpy.wait()
        ckv_c = ckv_buf[slot].reshape(tk, kvl).astype(BF16)
        krt_c = jnp.concatenate([kr_buf[slot, j] for j in range(cp)], axis=1).astype(BF16)
        return _lane_softmax_step(scores(ckv_c, _dot(qr, krt_c)), ckv_c, *carry)

    init = (jnp.full((nrow, 1), NEG, F32), jnp.zeros((nrow, 1), F32), jnp.zeros((nrow, kvl), F32))
    carry = lax.fori_loop(0, nchunk, body, init)

    pad = LANE - ts
    cnew = jnp.concatenate([cnew_ref[0], jnp.zeros((pad, kvl), F32)], axis=0).astype(BF16)
    rnew = jnp.concatenate([rnew_ref[0], jnp.zeros((pad, MLA_ROPE), F32)], axis=0).astype(BF16)
    s = scores(cnew, _dot_nt(qr, rnew))
    key = lax.broadcasted_iota(jnp.int32, s.shape, 1)
    tok = lax.broadcasted_iota(jnp.int32, s.shape, 0) % ts
    s = jnp.where(key <= tok, s, NEG)
    _, l, acc = _lane_softmax_step(s, cnew, *carry)
    lat = (acc / l).astype(BF16)
    out = jnp.zeros((ts, nh * MLA_V), F32)
    for h in range(nh):
        out = out + _dot(lat[h * ts:(h + 1) * ts, :], wuvp[h])
    o_ref[0] = out


def _mla_sample(page_table, q_s, ckv_new, kr_new, cache_ckv, cache_krope_t, w):
    bsz, npages = page_table.shape
    ts = q_s.shape[1]
    page, kvl = cache_ckv.shape[1:]
    cp = math.gcd(npages, MLA_SAMPLE_PAGES_PER_CHUNK)
    nh = MLA_HEADS
    grid_spec = pltpu.PrefetchScalarGridSpec(
        num_scalar_prefetch=1,
        grid=(bsz,),
        in_specs=[pl.BlockSpec((1, ts, nh * QPAD), lambda b, pt: (b, 0, 0)),
                  pl.BlockSpec((1, ts, kvl), lambda b, pt: (b, 0, 0)),
                  pl.BlockSpec((1, ts, MLA_ROPE), lambda b, pt: (b, 0, 0)),
                  pl.BlockSpec(memory_space=pl.ANY),
                  pl.BlockSpec(memory_space=pl.ANY),
                  pl.BlockSpec(w["wukt"].shape, lambda b, pt: (0, 0)),
                  pl.BlockSpec(w["gkn"].shape, lambda b, pt: (0, 0)),
                  pl.BlockSpec(w["wuv_pad"].shape, lambda b, pt: (0, 0, 0))],
        out_specs=pl.BlockSpec((1, ts, nh * MLA_V), lambda b, pt: (b, 0, 0)),
        scratch_shapes=[pltpu.VMEM((2, cp, page, kvl), F32),
                        pltpu.VMEM((2, cp, MLA_ROPE, page), F32),
                        pltpu.VMEM((nh * MLA_NOPE + nh * ts, kvl), BF16),
                        pltpu.SemaphoreType.DMA((2,)),
                        pltpu.SemaphoreType.DMA((2,))],
    )
    return pl.pallas_call(
        functools.partial(_mla_sample_kernel, cp),
        grid_spec=grid_spec,
        out_shape=jax.ShapeDtypeStruct((bsz, ts, nh * MLA_V), F32),
        compiler_params=pltpu.CompilerParams(dimension_semantics=("arbitrary",), vmem_limit_bytes=VMEM_LIMIT),
        name="mla_sample",
    )(page_table, q_s, ckv_new, kr_new, cache_ckv, cache_krope_t, w["wukt"], w["gkn"], w["wuv_pad"])


def _moba_sample_kernel(cp, pt_ref, q_ref, knew_ref, vnew_ref, ckt, cvt, o_ref,
                        kbuf, vbuf, s_all, k_sem, v_sem):
    b = pl.program_id(0)
    npages = pt_ref.shape[1]
    nchunk = npages // cp
    ts = q_ref.shape[1]
    wdt, page = kbuf.shape[2:]
    tk = cp * page
    bpc = tk // MOBA_BLOCK
    nb = npages * page // MOBA_BLOCK
    nrow = MOBA_HEADS * ts

    def kcopies(seq, c, slot):
        return _page_copies(pt_ref, seq, c, slot, cp, (ckt,), (kbuf,), (k_sem,))

    def vcopies(c, slot):
        return _page_copies(pt_ref, b, c, slot, cp, (cvt,), (vbuf,), (v_sem,))

    @pl.when(b == 0)
    def _():
        for cpy in kcopies(0, 0, 0):
            cpy.start()

    for cpy in vcopies(0, 0):
        cpy.start()

    mq = q_ref[0]
    lane = lax.broadcasted_iota(jnp.int32, (ts, wdt), 1)
    rows = []
    for kvh in range(MOBA_KV_HEADS):
        for g in range(MOBA_GROUP):
            rows.append(jnp.where(lane // HEAD_DIM == kvh, mq[:, g * wdt:(g + 1) * wdt], 0.0))
    qhi, qlo = _split_bf16(jnp.concatenate(rows, axis=0))
    q2 = jnp.concatenate([qhi, qlo], axis=0)
    blk_lane = lax.broadcasted_iota(jnp.int32, (nrow, LANE), 1)

    def pass1(c, carry):
        gsum, gmax = carry
        slot = c % 2

        @pl.when(c + 1 < nchunk)
        def _():
            for cpy in kcopies(b, c + 1, 1 - slot):
                cpy.start()

        for cpy in kcopies(b, c, slot):
            cpy.wait()
        kt = jnp.concatenate([kbuf[slot, j] for j in range(cp)], axis=1).astype(BF16)
        s2 = _dot(q2, kt)
        s = s2[:nrow] + s2[nrow:]
        s_all[:, pl.ds(pl.multiple_of(c * tk, tk), tk)] = s
        for jb in range(bpc):
            blk = s[:, jb * MOBA_BLOCK:(jb + 1) * MOBA_BLOCK]
            here = blk_lane == c * bpc + jb
            gsum = jnp.where(here, jnp.sum(blk, axis=-1, keepdims=True), gsum)
            gmax = jnp.where(here, jnp.max(blk, axis=-1, keepdims=True), gmax)
        return gsum, gmax

    gsum, gmax = lax.fori_loop(0, nchunk, pass1, (jnp.full((nrow, LANE), NEG, F32), jnp.full((nrow, LANE), NEG, F32)))

    @pl.when(b + 1 < pl.num_programs(0))
    def _():
        for cpy in kcopies(b + 1, 0, 0):
            cpy.start()

    sg = jnp.where(blk_lane < nb, gsum, NEG)
    sel = jnp.zeros(sg.shape, F32)
    for _ in range(MOBA_TOPK):
        mx = jnp.max(sg, axis=-1, keepdims=True)
        first = jnp.min(jnp.where(sg == mx, blk_lane, LANE), axis=-1, keepdims=True)
        sel = jnp.where((blk_lane == first) & (mx > 0.5 * NEG), 1.0, sel)
        sg = jnp.where(blk_lane == first, NEG, sg)

    pad = LANE - ts
    knew = jnp.concatenate([knew_ref[0], jnp.zeros((pad, wdt), F32)], axis=0).astype(BF16)
    vnew = jnp.concatenate([vnew_ref[0], jnp.zeros((pad, wdt), F32)], axis=0).astype(BF16)
    s2 = _dot_nt(q2, knew)
    key = lax.broadcasted_iota(jnp.int32, (nrow, LANE), 1)
    tok = lax.broadcasted_iota(jnp.int32, (nrow, LANE), 0) % ts
    s_new = jnp.where(key <= tok, s2[:nrow] + s2[nrow:], NEG)
    m = jnp.maximum(jnp.max(jnp.where(sel > 0.5, gmax, NEG), axis=-1, keepdims=True),
                    jnp.max(s_new, axis=-1, keepdims=True))

    def pass2(c, carry):
        l, acc = carry
        slot = c % 2

        @pl.when(c + 1 < nchunk)
        def _():
            for cpy in vcopies(c + 1, 1 - slot):
                cpy.start()

        for cpy in vcopies(c, slot):
            cpy.wait()
        vt = jnp.concatenate([vbuf[slot, j] for j in range(cp)], axis=1).astype(BF16)
        s = s_all[:, pl.ds(pl.multiple_of(c * tk, tk), tk)]
        cols = []
        for jb in range(bpc):
            on = jnp.sum(jnp.where(blk_lane == c * bpc + jb, sel, 0.0), axis=-1, keepdims=True) > 0.5
            cols.append(jnp.exp2(jnp.where(on, s[:, jb * MOBA_BLOCK:(jb + 1) * MOBA_BLOCK] - m, NEG)))
        p = jnp.concatenate(cols, axis=1)
        return l + jnp.sum(p, axis=-1, keepdims=True), acc + _dot_nt(p.astype(BF16), vt)

    l, acc = lax.fori_loop(0, nchunk, pass2, (jnp.zeros((nrow, 1), F32), jnp.zeros((nrow, wdt), F32)))
    p = jnp.exp2(s_new - m)
    l = l + jnp.sum(p, axis=-1, keepdims=True)
    acc = acc + _dot(p.astype(BF16), vnew)
    o = acc / l
    outs = []
    for g in range(MOBA_GROUP):
        og = jnp.zeros((ts, wdt), F32)
        for kvh in range(MOBA_KV_HEADS):
            r0 = (kvh * MOBA_GROUP + g) * ts
            og = og + jnp.where(lane // HEAD_DIM == kvh, o[r0:r0 + ts, :], 0.0)
        outs.append(og)
    o_ref[0] = jnp.concatenate(outs, axis=1)


def _moba_sample(page_table, mq_s, mk_new, mv_new, cache_kt, cache_vt):
    bsz, npages = page_table.shape
    ts = mq_s.shape[1]
    wdt, page = cache_kt.shape[1:]
    cp = math.gcd(npages // 2, MOBA_SAMPLE_PAGES_PER_CHUNK)
    past = npages * page
    nrow = MOBA_HEADS * ts
    assert (cp * page) % MOBA_BLOCK == 0 and (npages // cp) % 2 == 0 and past // MOBA_BLOCK <= LANE
    grid_spec = pltpu.PrefetchScalarGridSpec(
        num_scalar_prefetch=1,
        grid=(bsz,),
        in_specs=[pl.BlockSpec((1, ts, MOBA_HEADS * HEAD_DIM), lambda b, pt: (b, 0, 0)),
                  pl.BlockSpec((1, ts, wdt), lambda b, pt: (b, 0, 0)),
                  pl.BlockSpec((1, ts, wdt), lambda b, pt: (b, 0, 0)),
                  pl.BlockSpec(memory_space=pl.ANY),
                  pl.BlockSpec(memory_space=pl.ANY)],
        out_specs=pl.BlockSpec((1, ts, MOBA_HEADS * HEAD_DIM), lambda b, pt: (b, 0, 0)),
        scratch_shapes=[pltpu.VMEM((2, cp, wdt, page), F32),
                        pltpu.VMEM((2, cp, wdt, page), F32),
                        pltpu.VMEM((nrow, past), F32),
                        pltpu.SemaphoreType.DMA((2,)),
                        pltpu.SemaphoreType.DMA((2,))],
    )
    return pl.pallas_call(
        functools.partial(_moba_sample_kernel, cp),
        grid_spec=grid_spec,
        out_shape=jax.ShapeDtypeStruct((bsz, ts, MOBA_HEADS * HEAD_DIM), F32),
        compiler_params=pltpu.CompilerParams(dimension_semantics=("arbitrary",), vmem_limit_bytes=VMEM_LIMIT),
        name="moba_sample",
    )(page_table, mq_s, mk_new, mv_new, cache_kt, cache_vt)


def _prep_weights(g_attn, w_in, g_q_lora, g_kv_lora, w_uq, g_q_nope, g_q_rope, g_k_rope, w_uk, w_uv, g_k_nope,
                  g_moba_q, g_moba_k, w_oa, w_ob, w_out, g_ffn, w_group, b_group, w_expert_router,
                  b_expert_router, w_exp_gate, w_exp_up, w_exp_down):
    d = w_in.shape[0]
    ql = g_q_lora.shape[0]
    kvl = g_kv_lora.shape[0]
    nq = MOBA_HEADS * HEAD_DIM
    nkv = MOBA_KV_HEADS * HEAD_DIM
    sizes = (ql, kvl, MLA_ROPE, nq, nkv, nkv, d, d)
    assert sum(sizes) == w_in.shape[1]
    parts, o = [], 0
    for s in sizes:
        parts.append(w_in[:, o:o + s])
        o += s
    cq, ckv, kr, mq, mk, mv, ga, gb = parts
    sm = HEAD_DIM ** -0.5 * LOG2E
    sq = (MLA_NOPE + MLA_ROPE) ** -0.5 * LOG2E
    row = lambda v: v.reshape(1, -1).astype(F32)

    def w_in_layout(mqc):
        return jnp.concatenate(
            [cq, ckv, mqc, _swap_halves(mqc, HEAD_DIM), mk, _swap_halves(mk, HEAD_DIM), mv,
             _pad_lanes(kr, LANE), _pad_lanes(_swap_halves(kr, MLA_ROPE), LANE)], axis=1).astype(BF16)

    mq_gkd = mq.reshape(d, MOBA_KV_HEADS, MOBA_GROUP, HEAD_DIM).transpose(0, 2, 1, 3).reshape(d, nq)
    w = {"g_q_lora": g_q_lora, "g_kv_lora": g_kv_lora}
    w["w_in_p"] = w_in_layout(mq)
    w["w_in_s"] = w_in_layout(mq_gkd)
    w["gattn"] = row(g_attn)
    w["gql"] = row(g_q_lora)
    w["gkvl"] = row(g_kv_lora)
    w["gmq"] = row(jnp.tile(g_moba_q, MOBA_HEADS)) * sm
    w["gmqs"] = row(jnp.tile(_swap_halves(g_moba_q, HEAD_DIM), MOBA_HEADS)) * sm
    w["gmk"] = row(jnp.tile(g_moba_k, MOBA_KV_HEADS))
    w["gmks"] = row(jnp.tile(_swap_halves(g_moba_k, HEAD_DIM), MOBA_KV_HEADS))
    w["gkr"] = row(_pad_lanes(g_k_rope, LANE))
    w["gkrs"] = row(_pad_lanes(_swap_halves(g_k_rope, MLA_ROPE), LANE))
    w["g512"] = _group_matrix(nq, [(0, HEAD_DIM)], HEAD_DIM)
    w["g128"] = _group_matrix(LANE, [(0, MLA_ROPE)], LANE)
    w["g1024"] = _group_matrix(MLA_HEADS * QPAD, [(0, MLA_NOPE), (MLA_NOPE, MLA_ROPE)], QPAD)
    tail = QPAD - MLA_NOPE - MLA_ROPE
    w["icq"] = row(jnp.tile(jnp.concatenate([jnp.full((MLA_NOPE,), 1.0 / MLA_NOPE), jnp.full((MLA_ROPE,), 1.0 / MLA_ROPE),
                                             jnp.ones((tail,))]), MLA_HEADS))
    uq = w_uq.reshape(ql, MLA_HEADS, MLA_NOPE + MLA_ROPE)
    nope, rope = uq[..., :MLA_NOPE], uq[..., MLA_NOPE:]
    zt = jnp.zeros((ql, MLA_HEADS, tail), F32)
    w["wuqa"] = jnp.concatenate([nope, rope, zt], axis=-1).reshape(ql, -1).astype(BF16)
    w["wuqb"] = jnp.concatenate([jnp.zeros_like(nope), _swap_halves(rope, MLA_ROPE), zt], axis=-1).reshape(ql, -1).astype(BF16)
    z1 = jnp.zeros((tail,), F32)
    w["gqa"] = row(jnp.tile(jnp.concatenate([g_q_nope, g_q_rope, z1]), MLA_HEADS)) * sq
    w["gqb"] = row(jnp.tile(jnp.concatenate([jnp.zeros((MLA_NOPE,), F32), _swap_halves(g_q_rope, MLA_ROPE), z1]),
                            MLA_HEADS)) * sq
    w["wuk_pad"] = _pad_lanes(w_uk, QPAD).reshape(kvl, -1).astype(BF16)
    w["gkn"] = row(jnp.tile(_pad_lanes(g_k_nope, QPAD), MLA_HEADS))
    j = jnp.arange(LANE)[:, None]
    c = jnp.arange(MLA_HEADS * QPAD)[None, :]
    w["ekr"] = ((j < MLA_ROPE) & (c % QPAD == MLA_NOPE + j)).astype(BF16)
    w["wuv"] = w_uv.reshape(kvl, -1).astype(BF16)
    w["wukt"] = w_uk.reshape(kvl, -1).T.astype(BF16)
    head_of_col = jnp.arange(MLA_HEADS * MLA_V) // MLA_V
    w["wuv_pad"] = jnp.where(head_of_col[None, None, :] == jnp.arange(MLA_HEADS)[:, None, None],
                             w_uv.reshape(1, kvl, -1), 0.0).astype(BF16)

    amax = lambda v: jnp.max(jnp.abs(v))
    w["mla_bound"] = BF16_SLACK * sq * (MLA_NOPE * amax(g_q_nope) * amax(g_k_nope) + MLA_ROPE * amax(g_q_rope) * amax(g_k_rope))
    w["moba_bound"] = BF16_SLACK * sm * HEAD_DIM * amax(g_moba_q) * amax(g_moba_k)

    w["wgate"] = jnp.concatenate([ga, gb], axis=1).astype(BF16)
    w["woa"] = w_oa.astype(BF16)
    w["wob_p"] = w_ob.astype(BF16)
    w["wob_s"] = w_ob.reshape(MOBA_KV_HEADS, MOBA_GROUP, HEAD_DIM, -1).transpose(1, 0, 2, 3).reshape(nq, -1).astype(BF16)
    w["wout"] = w_out.astype(BF16)
    w["gffn"] = row(g_ffn)
    wr = _pad_lanes(jnp.concatenate([w_expert_router, w_group], axis=1), LANE)
    w["wr_hi"], w["wr_lo"] = _split_bf16(wr)
    w["br"] = row(_pad_lanes(jnp.concatenate([b_expert_router, b_group]), LANE))
    w["wg"] = w_exp_gate.astype(BF16)
    w["wu"] = w_exp_up.astype(BF16)
    w["wd"] = w_exp_down.astype(BF16)
    return w


def _row_tile(t, want):
    return want if t % want == 0 else t


def kernel(x_prompt, x_sample, cache_ckv, cache_krope, cache_k, cache_v, page_table, g_attn, w_in, g_q_lora, g_kv_lora, w_uq, g_q_nope, g_q_rope, g_k_rope, w_uk, w_uv, g_k_nope, g_moba_q, g_moba_k, w_oa, w_ob, w_out, g_ffn, w_group, b_group, w_expert_router, b_expert_router, w_exp_gate, w_exp_up, w_exp_down):
    w = _prep_weights(g_attn, w_in, g_q_lora, g_kv_lora, w_uq, g_q_nope, g_q_rope, g_k_rope, w_uk, w_uv, g_k_nope,
                      g_moba_q, g_moba_k, w_oa, w_ob, w_out, g_ffn, w_group, b_group, w_expert_router,
                      b_expert_router, w_exp_gate, w_exp_up, w_exp_down)
    bp, t, d = x_prompt.shape
    bs, ts, _ = x_sample.shape
    npool, page = cache_ckv.shape[:2]
    past = page_table.shape[1] * page
    assert bp == 1 and t % MOBA_BLOCK == 0
    assert past % MOBA_BLOCK == 0 and ts <= LANE and ts % 8 == 0

    xp = x_prompt.reshape(t, d)
    ckv_p, kr_p, mk_p, mv_p, qt, kp, vt, mqt, mkb, mvt = _features(
        xp, jnp.arange(t, dtype=jnp.int32), w, True, _row_tile(t, 256))
    tq = _row_tile(t, 512)
    oat = lax.cond(w["mla_bound"] <= EXP2_SAFE, functools.partial(_mla_prompt, False, tq, tq, ATTN_TILE_GROUP),
                   functools.partial(_mla_prompt, True, tq, tq, 1), qt, kp, vt)
    obt = lax.cond(w["moba_bound"] <= EXP2_SAFE, functools.partial(_moba_prompt, False, tq, ATTN_TILE_GROUP),
                   functools.partial(_moba_prompt, True, tq, 1), mqt, mkb, mvt, _block_means(mk_p))
    x1, hn, gates = _merge(xp, oat, obt, w, True, _row_tile(t, 256))
    y_p = _moe(x1, hn, gates, w, _row_tile(t, 1024))

    xs = x_sample.reshape(bs * ts, d)
    pos_s = past + jnp.tile(jnp.arange(ts, dtype=jnp.int32), bs)
    ckv_s, kr_s, mk_s, mv_s, q_s, mq_s = _features(xs, pos_s, w, False, _row_tile(bs * ts, 256))
    kvl = ckv_s.shape[1]
    nkv = MOBA_KV_HEADS * HEAD_DIM
    oa_s = _mla_sample(page_table, q_s.reshape(bs, ts, -1), ckv_s.reshape(bs, ts, kvl), kr_s.reshape(bs, ts, -1),
                       cache_ckv, jnp.transpose(cache_krope, (0, 2, 1)), w)
    ob_s = _moba_sample(page_table, mq_s.reshape(bs, ts, -1), mk_s.reshape(bs, ts, nkv), mv_s.reshape(bs, ts, nkv),
                        jnp.transpose(cache_k, (0, 2, 3, 1)).reshape(npool, nkv, page),
                        jnp.transpose(cache_v, (0, 2, 3, 1)).reshape(npool, nkv, page))
    x1s, hns, gates_s = _merge(xs, oa_s.reshape(bs * ts, -1), ob_s.reshape(bs * ts, -1), w, False,
                               _row_tile(bs * ts, 256))
    y_s = _moe(x1s, hns, gates_s, w, _row_tile(bs * ts, 1024))

    return (y_p.reshape(bp, t, d), y_s.reshape(bs, ts, d),
            ckv_p.reshape(bp, t, kvl), kr_p.reshape(bp, t, MLA_ROPE),
            mk_p.reshape(bp, t, MOBA_KV_HEADS, HEAD_DIM), mv_p.reshape(bp, t, MOBA_KV_HEADS, HEAD_DIM),
            ckv_s.reshape(bs, ts, kvl), kr_s.reshape(bs, ts, MLA_ROPE),
            mk_s.reshape(bs, ts, MOBA_KV_HEADS, HEAD_DIM), mv_s.reshape(bs, ts, MOBA_KV_HEADS, HEAD_DIM))
```

```python
import functools
import math

import jax
import jax.numpy as jnp
from jax import lax
from jax.experimental import pallas as pl
from jax.experimental.pallas import tpu as pltpu

MLA_HEADS = 8
MLA_NOPE = 64
MLA_ROPE = 32
MLA_V = 64
MOBA_HEADS = 8
MOBA_KV_HEADS = 4
MOBA_GROUP = MOBA_HEADS // MOBA_KV_HEADS
HEAD_DIM = 64
MOBA_BLOCK = 256
MOBA_TOPK = 3
N_GROUPS = 4
EXPERTS_PER_GROUP = 8
N_EXPERTS = N_GROUPS * EXPERTS_PER_GROUP
ROPE_THETA = 10000.0
NORM_EPS = 1e-6

LANE = 128
QPAD = LANE
LOG2E = 1.4426950408889634
NEG = -1e30
EXP2_SAFE = 40.0
BF16_SLACK = 1.02
ATTN_TILE_GROUP = 2
MLA_SAMPLE_PAGES_PER_CHUNK = 16
MOBA_SAMPLE_PAGES_PER_CHUNK = 8
MLA_SAMPLE_SLOTS = 3
MOBA_SAMPLE_SLOTS = 4
MOE_PACKED_ROWS = 320
VMEM_LIMIT = 56 * 1024 * 1024

F32 = jnp.float32
BF16 = jnp.bfloat16


def _dot(a, b):
    return jnp.dot(a, b, preferred_element_type=F32)


def _dot_nt(a, b):
    return lax.dot_general(a, b, (((1,), (1,)), ((), ())), preferred_element_type=F32)


def _split_bf16(x):
    hi = x.astype(BF16)
    lo = (x - hi.astype(F32)).astype(BF16)
    return hi, lo


def _group_ssq(x, g):
    hi, lo = _split_bf16(x * x)
    return _dot(hi, g) + _dot(lo, g)


def _rms(x):
    return x * lax.rsqrt(jnp.mean(x * x, axis=-1, keepdims=True) + NORM_EPS)


def _tile_lanes(x, n):
    return x if n == 1 else jnp.concatenate([x] * n, axis=1)


def _feat_kernel(prompt, dims, *refs):
    ql, kvl = dims
    (x_ref, gattn, w_in, gql, gkvl, gmq, gmqs, gmk, gmks, gkr, gkrs,
     c64, s64, c32, s32, cqt, sqt, g512, g128, g1024, icq, wuqa, wuqb, gqa, gqb) = refs[:25]
    rest = refs[25:]
    x = x_ref[...]
    h = (_rms(x) * gattn[...]).astype(BF16)
    z = _dot(h, w_in[...])
    o = 0
    cq = z[:, o:o + ql]; o += ql
    ckv = z[:, o:o + kvl]; o += kvl
    mq = z[:, o:o + 512]; o += 512
    mqs = z[:, o:o + 512]; o += 512
    mk = z[:, o:o + 256]; o += 256
    mks = z[:, o:o + 256]; o += 256
    mv = z[:, o:o + 256]; o += 256
    kr = z[:, o:o + 128]; o += 128
    krs = z[:, o:o + 128]; o += 128

    cqn = _rms(cq) * gql[...]
    ckvn = _rms(ckv) * gkvl[...]

    r = lax.rsqrt(_group_ssq(kr, g128[...]) * (1.0 / MLA_ROPE) + NORM_EPS)
    krn = r * (kr * gkr[...] * c32[...] + krs * gkrs[...] * s32[...])

    c2 = _tile_lanes(c64[...], 2)
    s2 = _tile_lanes(s64[...], 2)
    r = lax.rsqrt(_group_ssq(mk, g512[:256, :256]) * (1.0 / HEAD_DIM) + NORM_EPS)
    mkn = r * (mk * gmk[...] * c2 + mks * gmks[...] * s2)

    c4 = _tile_lanes(c2, 2)
    s4 = _tile_lanes(s2, 2)
    r = lax.rsqrt(_group_ssq(mq, g512[...]) * (1.0 / HEAD_DIM) + NORM_EPS)
    mqn = r * (mq * gmq[...] * c4 + mqs * gmqs[...] * s4)

    cqb = cqn.astype(BF16)
    qa = _dot(cqb, wuqa[...])
    qb = _dot(cqb, wuqb[...])
    r = lax.rsqrt(_group_ssq(qa, g1024[...]) * icq[...] + NORM_EPS)
    c8 = _tile_lanes(cqt[...], MLA_HEADS)
    s8 = _tile_lanes(sqt[...], MLA_HEADS)
    qn = r * (qa * gqa[...] * c8 + qb * gqb[...] * s8)

    if prompt:
        wuk, gkn, ekr, wuv = rest[:4]
        ckv_o, kr_o, mk_o, mv_o, qt_o, kp_o, vt_o, mqt_o, mkb_o, mvt_o = rest[4:]
    else:
        ckv_o, kr_o, mk_o, mv_o, q_o, mq_o = rest
    ckv_o[...] = ckvn
    kr_o[...] = krn[:, :MLA_ROPE]
    mk_o[...] = mkn
    mv_o[...] = mv
    if prompt:
        ckvb = ckvn.astype(BF16)
        kraw = _dot(ckvb, wuk[...])
        rk = lax.rsqrt(_group_ssq(kraw, g1024[...]) * icq[...] + NORM_EPS)
        kp = rk * kraw * gkn[...] + _dot(krn.astype(BF16), ekr[...])
        kp_o[...] = kp.astype(BF16)
        qt_o[...] = qn.T.astype(BF16)
        vt_o[...] = _dot(ckvb, wuv[...]).T.astype(BF16)
        mqt_o[...] = mqn.T.astype(BF16)
        mkb_o[...] = mkn.astype(BF16)
        mvt_o[...] = mv.T.astype(BF16)
    else:
        q_o[...] = qn
        mq_o[...] = mqn


def _swap_halves(w, hd):
    lead = w.shape[:-1]
    n = w.shape[-1] // hd
    w = w.reshape(lead + (n, 2, hd // 2))
    return jnp.flip(w, axis=-2).reshape(lead + (n * hd,))


def _pad_lanes(w, width):
    return jnp.pad(w, [(0, 0)] * (w.ndim - 1) + [(0, width - w.shape[-1])])


def _group_matrix(width, starts_sizes, period):
    lane = jnp.arange(width)
    blk = lane // period
    off = lane % period
    gid = jnp.full((width,), -1, jnp.int32)
    for k, (st, sz) in enumerate(starts_sizes):
        gid = jnp.where((off >= st) & (off < st + sz), k, gid)
    same = (blk[:, None] == blk[None, :]) & (gid[:, None] == gid[None, :]) & (gid[:, None] >= 0)
    return same.astype(BF16)


def _rope_tables(pos, half):
    inv = ROPE_THETA ** (-jnp.arange(half, dtype=F32) / half)
    ang = pos.astype(F32)[:, None] * inv[None, :]
    return jnp.cos(ang), jnp.sin(ang)


def _features(x2d, pos, w, prompt, tm):
    t, d = x2d.shape
    ql = w["g_q_lora"].shape[0]
    kvl = w["g_kv_lora"].shape[0]
    assert t % tm == 0
    cos32, sin32 = _rope_tables(pos, HEAD_DIM // 2)
    cos16, sin16 = _rope_tables(pos, MLA_ROPE // 2)
    c64 = jnp.concatenate([cos32, cos32] * 2, axis=1)
    s64 = jnp.concatenate([-sin32, sin32] * 2, axis=1)
    z96 = jnp.zeros((t, LANE - MLA_ROPE), F32)
    c32 = jnp.concatenate([cos16, cos16, z96], axis=1)
    s32 = jnp.concatenate([-sin16, sin16, z96], axis=1)
    z32 = jnp.zeros((t, QPAD - MLA_NOPE - MLA_ROPE), F32)
    cqt = jnp.concatenate([jnp.ones((t, MLA_NOPE), F32), cos16, cos16, z32], axis=1)
    sqt = jnp.concatenate([jnp.zeros((t, MLA_NOPE), F32), -sin16, sin16, z32], axis=1)

    row_args = [x2d, w["gattn"], w["w_in_p" if prompt else "w_in_s"], w["gql"], w["gkvl"],
                w["gmq"], w["gmqs"], w["gmk"], w["gmks"], w["gkr"], w["gkrs"]]
    tab_args = [c64, s64, c32, s32, cqt, sqt]
    mat_args = [w["g512"], w["g128"], w["g1024"], w["icq"], w["wuqa"], w["wuqb"], w["gqa"], w["gqb"]]
    if prompt:
        mat_args += [w["wuk_pad"], w["gkn"], w["ekr"], w["wuv"]]

    def full(a):
        return pl.BlockSpec(a.shape, lambda i: (0,) * a.ndim)

    in_specs = [pl.BlockSpec((tm, d), lambda i: (i, 0))] + [full(a) for a in row_args[1:]]
    in_specs += [pl.BlockSpec((tm, LANE), lambda i: (i, 0)) for _ in tab_args]
    in_specs += [full(a) for a in mat_args]

    def rows(width, dtype):
        return jax.ShapeDtypeStruct((t, width), dtype), pl.BlockSpec((tm, width), lambda i: (i, 0))

    def cols(height, dtype):
        return jax.ShapeDtypeStruct((height, t), dtype), pl.BlockSpec((height, tm), lambda i: (0, i))

    outs = [rows(kvl, F32), rows(MLA_ROPE, F32), rows(256, F32), rows(256, F32)]
    if prompt:
        outs += [cols(MLA_HEADS * QPAD, BF16), rows(MLA_HEADS * QPAD, BF16), cols(MLA_HEADS * MLA_V, BF16),
                 cols(512, BF16), rows(256, BF16), cols(256, BF16)]
    else:
        outs += [rows(MLA_HEADS * QPAD, F32), rows(512, F32)]
    return pl.pallas_call(
        functools.partial(_feat_kernel, prompt, (ql, kvl)),
        grid=(t // tm,),
        in_specs=in_specs,
        out_specs=[o[1] for o in outs],
        out_shape=[o[0] for o in outs],
        compiler_params=pltpu.CompilerParams(dimension_semantics=("parallel",), vmem_limit_bytes=VMEM_LIMIT),
        name="features_prompt" if prompt else "features_sample",
    )(*row_args, *tab_args, *mat_args)


def _softmax_init(online, dv, tq):
    if online:
        return (jnp.full((1, tq), NEG, F32), jnp.zeros((1, tq), F32), jnp.zeros((dv, tq), F32))
    return (jnp.zeros((8, tq), F32), jnp.zeros((dv, tq), F32))


def _softmax_step(online, s, vt, carry):
    if online:
        m, l, acc = carry
        m_new = jnp.maximum(m, jnp.max(s, axis=0, keepdims=True))
        alpha = jnp.exp2(m - m_new)
        p = jnp.exp2(s - m_new)
        l = alpha * l + jnp.sum(p, axis=0, keepdims=True)
        acc = alpha * acc + _dot(vt, p.astype(BF16))
        return m_new, l, acc
    l8, acc = carry
    p = jnp.exp2(s)
    l8 = l8 + jnp.sum(p.reshape(s.shape[0] // 8, 8, s.shape[1]), axis=0)
    acc = acc + _dot(vt, p.astype(BF16))
    return l8, acc


def _softmax_finish(online, carry):
    if online:
        _, l, acc = carry
    else:
        l8, acc = carry
        l = jnp.sum(l8, axis=0, keepdims=True)
    return acc / l


def _key_tile_loops(i, tq, tk, group, tile, carry):
    big = group * tk
    n_big = (i * tq) // big
    carry = lax.fori_loop(0, n_big, lambda j, c: tile(pl.multiple_of(j * big, big), big, c, False), carry)
    carry = lax.fori_loop(n_big * group, (i * tq) // tk,
                          lambda j, c: tile(pl.multiple_of(j * tk, tk), tk, c, False), carry)
    for u in range(tq // tk):
        carry = tile(pl.multiple_of(i * tq + u * tk, tk), tk, carry, True)
    return carry


def _mla_prompt_kernel(online, tq, tk, group, qt_ref, kp_ref, vt_ref, o_ref):
    i = pl.program_id(1)
    q = qt_ref[...]

    def tile(off, size, carry, masked):
        s = _dot(kp_ref[pl.ds(off, size), :], q)
        if masked:
            kpos = off + lax.broadcasted_iota(jnp.int32, (size, tq), 0)
            qpos = i * tq + lax.broadcasted_iota(jnp.int32, (size, tq), 1)
            s = jnp.where(kpos <= qpos, s, NEG)
        return _softmax_step(online, s, vt_ref[:, pl.ds(off, size)], carry)

    carry = _key_tile_loops(i, tq, tk, group, tile, _softmax_init(online, vt_ref.shape[0], tq))
    o_ref[...] = _softmax_finish(online, carry)


def _mla_prompt(online, tq, tk, group, qt, kp, vt):
    t = kp.shape[0]
    nh = MLA_HEADS
    return pl.pallas_call(
        functools.partial(_mla_prompt_kernel, online, tq, tk, group),
        grid=(nh, t // tq),
        in_specs=[pl.BlockSpec((QPAD, tq), lambda h, i: (h, i)),
                  pl.BlockSpec((t, QPAD), lambda h, i: (0, h)),
                  pl.BlockSpec((MLA_V, t), lambda h, i: (h, 0))],
        out_specs=pl.BlockSpec((MLA_V, tq), lambda h, i: (h, i)),
        out_shape=jax.ShapeDtypeStruct((nh * MLA_V, t), F32),
        compiler_params=pltpu.CompilerParams(dimension_semantics=("parallel", "arbitrary"),
                                             vmem_limit_bytes=VMEM_LIMIT),
        name="mla_prompt_online" if online else "mla_prompt",
    )(qt, kp, vt)


def _block_mean_kernel(k_ref, o_ref):
    k = k_ref[...]
    n = o_ref.shape[0]
    o_ref[...] = jnp.mean(k.reshape(n, MOBA_BLOCK, k.shape[1]), axis=1)


def _block_means(k2d):
    t, wdt = k2d.shape
    nb = t // MOBA_BLOCK
    per = 8 if nb % 8 == 0 else nb
    return pl.pallas_call(
        _block_mean_kernel,
        grid=(nb // per,),
        in_specs=[pl.BlockSpec((per * MOBA_BLOCK, wdt), lambda i: (i, 0))],
        out_specs=pl.BlockSpec((per, wdt), lambda i: (i, 0)),
        out_shape=jax.ShapeDtypeStruct((nb, wdt), F32),
        compiler_params=pltpu.CompilerParams(dimension_semantics=("parallel",)),
        name="moba_block_means",
    )(k2d)


def _select_topk_rows(s, limit):
    nb = s.shape[0]
    row = lax.broadcasted_iota(jnp.int32, s.shape, 0)
    s = jnp.where(row < limit, s, NEG)
    sel = jnp.zeros(s.shape, F32)
    for _ in range(MOBA_TOPK):
        mx = jnp.max(s, axis=0, keepdims=True)
        first = jnp.min(jnp.where(s == mx, row, nb), axis=0, keepdims=True)
        hit = (row == first) & (mx > 0.5 * NEG)
        sel = jnp.where(hit, 1.0, sel)
        s = jnp.where(row == first, NEG, s)
    return sel


def _moba_prompt_kernel(online, tq, group, q_ref, k_ref, vt_ref, mean_ref, o_ref, sel_ref):
    h = pl.program_id(0)
    i = pl.program_id(1)
    tk = MOBA_BLOCK
    odd = (h // MOBA_GROUP) % 2
    q64 = q_ref[...]
    zero = jnp.zeros_like(q64)
    q = jnp.concatenate([jnp.where(odd == 0, q64, zero), jnp.where(odd == 1, q64, zero)], axis=0)
    qpos1 = i * tq + lax.broadcasted_iota(jnp.int32, (1, tq), 1)
    cur1 = qpos1 // MOBA_BLOCK
    mhi, mlo = _split_bf16(mean_ref[...])
    sel_ref[...] = _select_topk_rows(_dot(mhi, q) + _dot(mlo, q), cur1)

    def tile(off, size, carry, own):
        s = _dot(k_ref[pl.ds(off, size), :], q)
        n0 = off // tk
        slabs = []
        for u in range(size // tk):
            su = s[u * tk:(u + 1) * tk]
            picked = jnp.where(sel_ref[pl.ds(n0 + u, 1), :] > 0.0, su, NEG)
            if own:
                kpos = off + u * tk + lax.broadcasted_iota(jnp.int32, (tk, tq), 0)
                qpos = i * tq + lax.broadcasted_iota(jnp.int32, (tk, tq), 1)
                picked = jnp.where(n0 + u == cur1, jnp.where(kpos <= qpos, su, NEG), picked)
            slabs.append(picked)
        s = slabs[0] if len(slabs) == 1 else jnp.concatenate(slabs, axis=0)
        return _softmax_step(online, s, vt_ref[:, pl.ds(off, size)], carry)

    carry = _key_tile_loops(i, tq, tq, group, tile, _softmax_init(online, vt_ref.shape[0], tq))
    o_ref[...] = _softmax_finish(online, carry)


def _moba_prompt(online, tq, group, mqt, mkb, mvt, means):
    t = mkb.shape[0]
    nb = means.shape[0]
    g = MOBA_GROUP
    return pl.pallas_call(
        functools.partial(_moba_prompt_kernel, online, tq, group),
        grid=(MOBA_HEADS, t // tq),
        in_specs=[pl.BlockSpec((HEAD_DIM, tq), lambda h, i: (h, i)),
                  pl.BlockSpec((t, LANE), lambda h, i: (0, h // (2 * g))),
                  pl.BlockSpec((HEAD_DIM, t), lambda h, i: (h // g, 0)),
                  pl.BlockSpec((nb, LANE), lambda h, i: (0, h // (2 * g)))],
        out_specs=pl.BlockSpec((HEAD_DIM, tq), lambda h, i: (h, i)),
        out_shape=jax.ShapeDtypeStruct((MOBA_HEADS * HEAD_DIM, t), F32),
        scratch_shapes=[pltpu.VMEM((nb, tq), F32)],
        compiler_params=pltpu.CompilerParams(dimension_semantics=("parallel", "arbitrary"),
                                             vmem_limit_bytes=VMEM_LIMIT),
        name="moba_prompt_online" if online else "moba_prompt",
    )(mqt, mkb, mvt, means)


def _merge_kernel(transposed, x_ref, oa_ref, ob_ref, gattn, wgate, woa, wob, wout, gffn, wr_hi, wr_lo, br,
                  x1_o, hn_o, gates_o):
    x = x_ref[...]
    d = x.shape[1]
    h = (_rms(x) * gattn[...]).astype(BF16)
    gate = jax.nn.sigmoid(_dot(h, wgate[...]))
    oa = oa_ref[...]
    ob = ob_ref[...]
    if transposed:
        oa = oa.T
        ob = ob.T
    mix = gate[:, :d] * _dot(oa.astype(BF16), woa[...]) + gate[:, d:] * _dot(ob.astype(BF16), wob[...])
    x1 = x + _dot(mix.astype(BF16), wout[...])
    x1_o[...] = x1
    hn = _rms(x1) * gffn[...]
    hn_o[...] = hn.astype(BF16)

    hhi, hlo = _split_bf16(hn)
    lg = _dot(hhi, wr_hi[...]) + _dot(hlo, wr_hi[...]) + _dot(hhi, wr_lo[...]) + br[...]
    lane = lax.broadcasted_iota(jnp.int32, lg.shape, 1)
    is_g = (lane >= N_EXPERTS) & (lane < N_EXPERTS + N_GROUPS)
    gl = jnp.where(is_g, lg, NEG)
    gmax = jnp.max(gl, axis=-1, keepdims=True)
    gsum = jnp.sum(jnp.where(is_g, jnp.exp(gl - gmax), 0.0), axis=-1, keepdims=True)
    p_g = 1.0 / gsum
    g_idx = jnp.min(jnp.where(gl == gmax, lane - N_EXPERTS, LANE), axis=-1, keepdims=True)
    in_grp = (lane < N_EXPERTS) & (lane // EXPERTS_PER_GROUP == g_idx)
    el = jnp.where(in_grp, lg, NEG)
    v1 = jnp.max(el, axis=-1, keepdims=True)
    i1 = jnp.min(jnp.where(el == v1, lane, LANE), axis=-1, keepdims=True)
    el2 = jnp.where(lane == i1, NEG, el)
    v2 = jnp.max(el2, axis=-1, keepdims=True)
    i2 = jnp.min(jnp.where(el2 == v2, lane, LANE), axis=-1, keepdims=True)
    e2 = jnp.exp(v2 - v1)
    w1 = p_g / (1.0 + e2)
    w2 = p_g * e2 / (1.0 + e2)
    gates_o[...] = jnp.where(lane == i1, w1, 0.0) + jnp.where(lane == i2, w2, 0.0)


def _merge(x2d, oa, ob, w, transposed, tm):
    t, d = x2d.shape
    da = MLA_HEADS * MLA_V
    db = MOBA_HEADS * HEAD_DIM
    wargs = [w["gattn"], w["wgate"], w["woa"], w["wob_p" if transposed else "wob_s"], w["wout"], w["gffn"],
             w["wr_hi"], w["wr_lo"], w["br"]]

    def full(a):
        return pl.BlockSpec(a.shape, lambda i: (0,) * a.ndim)

    if transposed:
        o_specs = [pl.BlockSpec((da, tm), lambda i: (0, i)), pl.BlockSpec((db, tm), lambda i: (0, i))]
    else:
        o_specs = [pl.BlockSpec((tm, da), lambda i: (i, 0)), pl.BlockSpec((tm, db), lambda i: (i, 0))]
    return pl.pallas_call(
        functools.partial(_merge_kernel, transposed),
        grid=(t // tm,),
        in_specs=[pl.BlockSpec((tm, d), lambda i: (i, 0))] + o_specs + [full(a) for a in wargs],
        out_specs=[pl.BlockSpec((tm, d), lambda i: (i, 0)), pl.BlockSpec((tm, d), lambda i: (i, 0)),
                   pl.BlockSpec((tm, LANE), lambda i: (i, 0))],
        out_shape=[jax.ShapeDtypeStruct((t, d), F32), jax.ShapeDtypeStruct((t, d), BF16),
                   jax.ShapeDtypeStruct((t, LANE), F32)],
        compiler_params=pltpu.CompilerParams(dimension_semantics=("parallel",), vmem_limit_bytes=VMEM_LIMIT),
        name="merge_prompt" if transposed else "merge_sample",
    )(x2d, oa, ob, *wargs)


def _expert_ffn(x, gcol, wg_ref, wu_ref, wd_ref):
    a = _dot(x, wg_ref[0])
    hid = a * jax.nn.sigmoid(a) * _dot(x, wu_ref[0])
    return _dot((hid * gcol).astype(BF16), wd_ref[0])


def _moe_kernel(fast, x1_ref, hn_ref, gates_ref, ltri_ref, wg_ref, wu_ref, wd_ref, o_ref,
                xg, gg, yg, scat, cnt_s):
    g = pl.program_id(1)
    e = pl.program_id(2)
    tm = hn_ref.shape[0]
    lane = lax.broadcasted_iota(jnp.int32, (tm, LANE), 1)

    @pl.when((g == 0) & (e == 0))
    def _():
        o_ref[...] = x1_ref[...]

    @pl.when(e == 0)
    def _():
        gates = gates_ref[...]
        in_grp = jnp.sum(jnp.where(lane // EXPERTS_PER_GROUP == g, gates, 0.0), axis=-1, keepdims=True) > 0.0
        ind = jnp.where(in_grp, 1.0, 0.0) + jnp.zeros((tm, LANE), F32)
        ltri = ltri_ref[...]
        rank_col = _dot(ltri, ind.astype(BF16))
        rank_row = _dot_nt(ind.T.astype(BF16), ltri)
        cnt_s[0] = jnp.max(rank_col).astype(jnp.int32)
        slot_row = jnp.where(ind.T[0:1, :] > 0.5, rank_row[0:1, :] - 1.0, -1.0)
        slot_col = jnp.where(ind[:, 0:1] > 0.5, rank_col[:, 0:1] - 1.0, -1.0)
        r_id = lax.broadcasted_iota(jnp.int32, (fast, tm), 0).astype(F32)
        pack = jnp.where(r_id == slot_row, 1.0, 0.0).astype(BF16)
        c_id = lax.broadcasted_iota(jnp.int32, (tm, fast), 1).astype(F32)
        scat[...] = jnp.where(c_id == slot_col, 1.0, 0.0).astype(BF16)
        xg[...] = _dot(pack, hn_ref[...]).astype(BF16)
        ghi, glo = _split_bf16(gates)
        gg[...] = _dot(pack, ghi) + _dot(pack, glo)
        yg[...] = jnp.zeros_like(yg)

    eg = g * EXPERTS_PER_GROUP + e
    packed = cnt_s[0] <= fast

    @pl.when(packed)
    def _():
        lane_p = lax.broadcasted_iota(jnp.int32, (fast, LANE), 1)
        gcol = jnp.sum(jnp.where(lane_p == eg, gg[...], 0.0), axis=-1, keepdims=True)
        yg[...] += _expert_ffn(xg[...], gcol, wg_ref, wu_ref, wd_ref)

    @pl.when(packed & (e == EXPERTS_PER_GROUP - 1))
    def _():
        yhi, ylo = _split_bf16(yg[...])
        o_ref[...] += _dot(scat[...], yhi) + _dot(scat[...], ylo)

    @pl.when(jnp.logical_not(packed))
    def _():
        gcol = jnp.sum(jnp.where(lane == eg, gates_ref[...], 0.0), axis=-1, keepdims=True)
        o_ref[...] += _expert_ffn(hn_ref[...], gcol, wg_ref, wu_ref, wd_ref)


def _moe(x1, hn, gates, w, tm):
    t, d = x1.shape
    ne, _, f = w["wg"].shape
    fast = min(MOE_PACKED_ROWS, tm)
    ltri = (jnp.arange(tm)[:, None] >= jnp.arange(tm)[None, :]).astype(BF16)
    wmap = lambda i, g, e: (g * EXPERTS_PER_GROUP + e, 0, 0)
    return pl.pallas_call(
        functools.partial(_moe_kernel, fast),
        grid=(t // tm, N_GROUPS, EXPERTS_PER_GROUP),
        in_specs=[pl.BlockSpec((tm, d), lambda i, g, e: (i, 0)),
                  pl.BlockSpec((tm, d), lambda i, g, e: (i, 0)),
                  pl.BlockSpec((tm, LANE), lambda i, g, e: (i, 0)),
                  pl.BlockSpec((tm, tm), lambda i, g, e: (0, 0)),
                  pl.BlockSpec((1, d, f), wmap),
                  pl.BlockSpec((1, d, f), wmap),
                  pl.BlockSpec((1, f, d), wmap)],
        out_specs=pl.BlockSpec((tm, d), lambda i, g, e: (i, 0)),
        out_shape=jax.ShapeDtypeStruct((t, d), F32),
        scratch_shapes=[pltpu.VMEM((fast, d), BF16), pltpu.VMEM((fast, LANE), F32), pltpu.VMEM((fast, d), F32),
                        pltpu.VMEM((tm, fast), BF16), pltpu.SMEM((1,), jnp.int32)],
        compiler_params=pltpu.CompilerParams(dimension_semantics=("parallel", "arbitrary", "arbitrary"),
                                             vmem_limit_bytes=VMEM_LIMIT),
        name="moe_experts",
    )(x1, hn, gates, ltri, w["wg"], w["wu"], w["wd"])


def _page_copies(pt_ref, b, chunk, slot, pages_per_chunk, srcs, bufs, sems):
    out = []
    for j in range(pages_per_chunk):
        pg = pt_ref[b, chunk * pages_per_chunk + j]
        for src, buf, sem in zip(srcs, bufs, sems):
            out.append(pltpu.make_async_copy(src.at[pg], buf.at[slot, j], sem.at[slot]))
    return out


def _stream_prime(total, nslot, nchunk, copies):
    for a in range(nslot - 1):
        @pl.when(a < total)
        def _():
            for cpy in copies(a // nchunk, a % nchunk, a % nslot):
                cpy.start()


def _stream_advance(g, total, nslot, nchunk, copies):
    ahead = g + (nslot - 1)

    @pl.when(ahead < total)
    def _():
        for cpy in copies(ahead // nchunk, ahead % nchunk, ahead % nslot):
            cpy.start()

    for cpy in copies(g // nchunk, g % nchunk, g % nslot):
        cpy.wait()
    return g % nslot


def _lane_softmax_step(s, v, m, l, acc):
    m_new = jnp.maximum(m, jnp.max(s, axis=-1, keepdims=True))
    alpha = jnp.exp2(m - m_new)
    p = jnp.exp2(s - m_new)
    l = alpha * l + jnp.sum(p, axis=-1, keepdims=True)
    acc = alpha * acc + _dot(p.astype(BF16), v)
    return m_new, l, acc


def _mla_sample_kernel(cp, pt_ref, q_ref, cnew_ref, rnew_ref, cckv, ckr, wukt, gkn, wuvp, o_ref,
                       ckv_buf, kr_buf, lhs, ckv_sem, kr_sem):
    b = pl.program_id(0)
    npages = pt_ref.shape[1]
    nchunk = npages // cp
    ts = q_ref.shape[1]
    nh = MLA_HEADS
    nrow = nh * ts
    page = ckv_buf.shape[2]
    tk = cp * page
    kvl = ckv_buf.shape[3]
    hk = nh * MLA_NOPE

    def copies(seq, c, slot):
        return _page_copies(pt_ref, seq, c, slot, cp, (cckv, ckr), (ckv_buf, kr_buf), (ckv_sem, kr_sem))

    nslot = ckv_buf.shape[0]
    total = pl.num_programs(0) * nchunk

    @pl.when(b == 0)
    def _():
        _stream_prime(total, nslot, nchunk, copies)

    q = q_ref[0]
    qg = (q * gkn[...]).astype(BF16)
    qroll = pltpu.roll(q, MLA_HEADS * QPAD - MLA_NOPE, axis=1)
    lhs[0:hk, :] = wukt[...]
    qr_rows = []
    for h in range(nh):
        qh = qg[:, h * QPAD:h * QPAD + MLA_NOPE]
        lhs[hk + h * ts:hk + (h + 1) * ts, :] = _dot(qh, wukt[h * MLA_NOPE:(h + 1) * MLA_NOPE, :]).astype(BF16)
        qr_rows.append(qroll[:, h * QPAD:h * QPAD + MLA_ROPE])
    qr = jnp.concatenate(qr_rows, axis=0).astype(BF16)

    def scores(ckv_c, rope_scores):
        n = ckv_c.shape[0]
        r_all = _dot_nt(lhs[...], ckv_c)
        kn = r_all[:hk].reshape(nh, MLA_NOPE, n)
        rinv = lax.rsqrt(jnp.sum(kn * kn, axis=1) * (1.0 / MLA_NOPE) + NORM_EPS)
        s_abs = r_all[hk:].reshape(nh, ts, n) * rinv[:, None, :]
        return s_abs.reshape(nrow, n) + rope_scores

    def body(c, carry):
        slot = _stream_advance(b * nchunk + c, total, nslot, nchunk, copies)
        ckv_c = ckv_buf[slot].reshape(tk, kvl).astype(BF16)
        krt_c = jnp.concatenate([kr_buf[slot, j] for j in range(cp)], axis=1).astype(BF16)
        return _lane_softmax_step(scores(ckv_c, _dot(qr, krt_c)), ckv_c, *carry)

    init = (jnp.full((nrow, 1), NEG, F32), jnp.zeros((nrow, 1), F32), jnp.zeros((nrow, kvl), F32))
    carry = lax.fori_loop(0, nchunk, body, init)

    pad = LANE - ts
    cnew = jnp.concatenate([cnew_ref[0], jnp.zeros((pad, kvl), F32)], axis=0).astype(BF16)
    rnew = jnp.concatenate([rnew_ref[0], jnp.zeros((pad, MLA_ROPE), F32)], axis=0).astype(BF16)
    s = scores(cnew, _dot_nt(qr, rnew))
    key = lax.broadcasted_iota(jnp.int32, s.shape, 1)
    tok = lax.broadcasted_iota(jnp.int32, s.shape, 0) % ts
    s = jnp.where(key <= tok, s, NEG)
    _, l, acc = _lane_softmax_step(s, cnew, *carry)
    lat = (acc / l).astype(BF16)
    out = jnp.zeros((ts, nh * MLA_V), F32)
    for h in range(nh):
        out = out + _dot(lat[h * ts:(h + 1) * ts, :], wuvp[h])
    o_ref[0] = out


def _mla_sample(page_table, q_s, ckv_new, kr_new, cache_ckv, cache_krope_t, w):
    bsz, npages = page_table.shape
    ts = q_s.shape[1]
    page, kvl = cache_ckv.shape[1:]
    cp = math.gcd(npages, MLA_SAMPLE_PAGES_PER_CHUNK)
    nh = MLA_HEADS
    grid_spec = pltpu.PrefetchScalarGridSpec(
        num_scalar_prefetch=1,
        grid=(bsz,),
        in_specs=[pl.BlockSpec((1, ts, nh * QPAD), lambda b, pt: (b, 0, 0)),
                  pl.BlockSpec((1, ts, kvl), lambda b, pt: (b, 0, 0)),
                  pl.BlockSpec((1, ts, MLA_ROPE), lambda b, pt: (b, 0, 0)),
                  pl.BlockSpec(memory_space=pl.ANY),
                  pl.BlockSpec(memory_space=pl.ANY),
                  pl.BlockSpec(w["wukt"].shape, lambda b, pt: (0, 0)),
                  pl.BlockSpec(w["gkn"].shape, lambda b, pt: (0, 0)),
                  pl.BlockSpec(w["wuv_pad"].shape, lambda b, pt: (0, 0, 0))],
        out_specs=pl.BlockSpec((1, ts, nh * MLA_V), lambda b, pt: (b, 0, 0)),
        scratch_shapes=[pltpu.VMEM((MLA_SAMPLE_SLOTS, cp, page, kvl), F32),
                        pltpu.VMEM((MLA_SAMPLE_SLOTS, cp, MLA_ROPE, page), F32),
                        pltpu.VMEM((nh * MLA_NOPE + nh * ts, kvl), BF16),
                        pltpu.SemaphoreType.DMA((MLA_SAMPLE_SLOTS,)),
                        pltpu.SemaphoreType.DMA((MLA_SAMPLE_SLOTS,))],
    )
    return pl.pallas_call(
        functools.partial(_mla_sample_kernel, cp),
        grid_spec=grid_spec,
        out_shape=jax.ShapeDtypeStruct((bsz, ts, nh * MLA_V), F32),
        compiler_params=pltpu.CompilerParams(dimension_semantics=("arbitrary",), vmem_limit_bytes=VMEM_LIMIT),
        name="mla_sample",
    )(page_table, q_s, ckv_new, kr_new, cache_ckv, cache_krope_t, w["wukt"], w["gkn"], w["wuv_pad"])


def _moba_sample_kernel(cp, pt_ref, q_ref, knew_ref, vnew_ref, ckt, cvt, o_ref,
                        kbuf, vbuf, s_all, k_sem, v_sem):
    b = pl.program_id(0)
    npages = pt_ref.shape[1]
    nchunk = npages // cp
    ts = q_ref.shape[1]
    wdt, page = kbuf.shape[2:]
    tk = cp * page
    bpc = tk // MOBA_BLOCK
    nb = npages * page // MOBA_BLOCK
    nrow = MOBA_HEADS * ts

    def kcopies(seq, c, slot):
        return _page_copies(pt_ref, seq, c, slot, cp, (ckt,), (kbuf,), (k_sem,))

    def vcopies(seq, c, slot):
        return _page_copies(pt_ref, seq, c, slot, cp, (cvt,), (vbuf,), (v_sem,))

    nslot = kbuf.shape[0]
    total = pl.num_programs(0) * nchunk

    @pl.when(b == 0)
    def _():
        _stream_prime(total, nslot, nchunk, kcopies)
        _stream_prime(total, nslot, nchunk, vcopies)

    mq = q_ref[0]
    lane = lax.broadcasted_iota(jnp.int32, (ts, wdt), 1)
    rows = []
    for kvh in range(MOBA_KV_HEADS):
        for g in range(MOBA_GROUP):
            rows.append(jnp.where(lane // HEAD_DIM == kvh, mq[:, g * wdt:(g + 1) * wdt], 0.0))
    qhi, qlo = _split_bf16(jnp.concatenate(rows, axis=0))
    q2 = jnp.concatenate([qhi, qlo], axis=0)
    blk_lane = lax.broadcasted_iota(jnp.int32, (nrow, LANE), 1)

    def pass1(c, carry):
        gsum, gmax = carry
        slot = _stream_advance(b * nchunk + c, total, nslot, nchunk, kcopies)
        kt =jnp.concatenate([kbuf[slot, j] for j in range(cp)], axis=1).astype(BF16)
        s2 = _dot(q2, kt)
        s = s2[:nrow] + s2[nrow:]
        s_all[:, pl.ds(pl.multiple_of(c * tk, tk), tk)] = s
        for jb in range(bpc):
            blk = s[:, jb * MOBA_BLOCK:(jb + 1) * MOBA_BLOCK]
            here = blk_lane == c * bpc + jb
            gsum = jnp.where(here, jnp.sum(blk, axis=-1, keepdims=True), gsum)
            gmax = jnp.where(here, jnp.max(blk, axis=-1, keepdims=True), gmax)
        return gsum, gmax

    gsum, gmax = lax.fori_loop(0, nchunk, pass1, (jnp.full((nrow, LANE), NEG, F32), jnp.full((nrow, LANE), NEG, F32)))

    sg =jnp.where(blk_lane < nb, gsum, NEG)
    sel = jnp.zeros(sg.shape, F32)
    for _ in range(MOBA_TOPK):
        mx = jnp.max(sg, axis=-1, keepdims=True)
        first = jnp.min(jnp.where(sg == mx, blk_lane, LANE), axis=-1, keepdims=True)
        sel = jnp.where((blk_lane == first) & (mx > 0.5 * NEG), 1.0, sel)
        sg = jnp.where(blk_lane == first, NEG, sg)

    pad = LANE - ts
    knew = jnp.concatenate([knew_ref[0], jnp.zeros((pad, wdt), F32)], axis=0).astype(BF16)
    vnew = jnp.concatenate([vnew_ref[0], jnp.zeros((pad, wdt), F32)], axis=0).astype(BF16)
    s2 = _dot_nt(q2, knew)
    key = lax.broadcasted_iota(jnp.int32, (nrow, LANE), 1)
    tok = lax.broadcasted_iota(jnp.int32, (nrow, LANE), 0) % ts
    s_new = jnp.where(key <= tok, s2[:nrow] + s2[nrow:], NEG)
    m = jnp.maximum(jnp.max(jnp.where(sel > 0.5, gmax, NEG), axis=-1, keepdims=True),
                    jnp.max(s_new, axis=-1, keepdims=True))

    def pass2(c, carry):
        l, acc = carry
        slot = _stream_advance(b * nchunk + c, total, nslot, nchunk, vcopies)
        vt =jnp.concatenate([vbuf[slot, j] for j in range(cp)], axis=1).astype(BF16)
        s = s_all[:, pl.ds(pl.multiple_of(c * tk, tk), tk)]
        cols = []
        for jb in range(bpc):
            on = jnp.sum(jnp.where(blk_lane == c * bpc + jb, sel, 0.0), axis=-1, keepdims=True) > 0.5
            cols.append(jnp.exp2(jnp.where(on, s[:, jb * MOBA_BLOCK:(jb + 1) * MOBA_BLOCK] - m, NEG)))
        p = jnp.concatenate(cols, axis=1)
        return l + jnp.sum(p, axis=-1, keepdims=True), acc + _dot_nt(p.astype(BF16), vt)

    l, acc = lax.fori_loop(0, nchunk, pass2, (jnp.zeros((nrow, 1), F32), jnp.zeros((nrow, wdt), F32)))
    p = jnp.exp2(s_new - m)
    l = l + jnp.sum(p, axis=-1, keepdims=True)
    acc = acc + _dot(p.astype(BF16), vnew)
    o = acc / l
    outs = []
    for g in range(MOBA_GROUP):
        og = jnp.zeros((ts, wdt), F32)
        for kvh in range(MOBA_KV_HEADS):
            r0 = (kvh * MOBA_GROUP + g) * ts
            og = og + jnp.where(lane // HEAD_DIM == kvh, o[r0:r0 + ts, :], 0.0)
        outs.append(og)
    o_ref[0] = jnp.concatenate(outs, axis=1)


def _moba_sample(page_table, mq_s, mk_new, mv_new, cache_kt, cache_vt):
    bsz, npages = page_table.shape
    ts = mq_s.shape[1]
    wdt, page = cache_kt.shape[1:]
    cp = math.gcd(npages, MOBA_SAMPLE_PAGES_PER_CHUNK)
    past = npages * page
    nrow = MOBA_HEADS * ts
    assert (cp * page) % MOBA_BLOCK == 0 and past // MOBA_BLOCK <= LANE
    grid_spec = pltpu.PrefetchScalarGridSpec(
        num_scalar_prefetch=1,
        grid=(bsz,),
        in_specs=[pl.BlockSpec((1, ts, MOBA_HEADS * HEAD_DIM), lambda b, pt: (b, 0, 0)),
                  pl.BlockSpec((1, ts, wdt), lambda b, pt: (b, 0, 0)),
                  pl.BlockSpec((1, ts, wdt), lambda b, pt: (b, 0, 0)),
                  pl.BlockSpec(memory_space=pl.ANY),
                  pl.BlockSpec(memory_space=pl.ANY)],
        out_specs=pl.BlockSpec((1, ts, MOBA_HEADS * HEAD_DIM), lambda b, pt: (b, 0, 0)),
        scratch_shapes=[pltpu.VMEM((MOBA_SAMPLE_SLOTS, cp, wdt, page), F32),
                        pltpu.VMEM((MOBA_SAMPLE_SLOTS, cp, wdt, page), F32),
                        pltpu.VMEM((nrow, past), F32),
                        pltpu.SemaphoreType.DMA((MOBA_SAMPLE_SLOTS,)),
                        pltpu.SemaphoreType.DMA((MOBA_SAMPLE_SLOTS,))],
    )
    return pl.pallas_call(
        functools.partial(_moba_sample_kernel, cp),
        grid_spec=grid_spec,
        out_shape=jax.ShapeDtypeStruct((bsz, ts, MOBA_HEADS * HEAD_DIM), F32),
        compiler_params=pltpu.CompilerParams(dimension_semantics=("arbitrary",), vmem_limit_bytes=VMEM_LIMIT),
        name="moba_sample",
    )(page_table, mq_s, mk_new, mv_new, cache_kt, cache_vt)


def _prep_weights(g_attn, w_in, g_q_lora, g_kv_lora, w_uq, g_q_nope, g_q_rope, g_k_rope, w_uk, w_uv, g_k_nope,
                  g_moba_q, g_moba_k, w_oa, w_ob, w_out, g_ffn, w_group, b_group, w_expert_router,
                  b_expert_router, w_exp_gate, w_exp_up, w_exp_down):
    d = w_in.shape[0]
    ql = g_q_lora.shape[0]
    kvl = g_kv_lora.shape[0]
    nq = MOBA_HEADS * HEAD_DIM
    nkv = MOBA_KV_HEADS * HEAD_DIM
    sizes = (ql, kvl, MLA_ROPE, nq, nkv, nkv, d, d)
    assert sum(sizes) == w_in.shape[1]
    parts, o = [], 0
    for s in sizes:
        parts.append(w_in[:, o:o + s])
        o += s
    cq, ckv, kr, mq, mk, mv, ga, gb = parts
    sm = HEAD_DIM ** -0.5 * LOG2E
    sq = (MLA_NOPE + MLA_ROPE) ** -0.5 * LOG2E
    row = lambda v: v.reshape(1, -1).astype(F32)

    def w_in_layout(mqc):
        return jnp.concatenate(
            [cq, ckv, mqc, _swap_halves(mqc, HEAD_DIM), mk, _swap_halves(mk, HEAD_DIM), mv,
             _pad_lanes(kr, LANE), _pad_lanes(_swap_halves(kr, MLA_ROPE), LANE)], axis=1).astype(BF16)

    mq_gkd = mq.reshape(d, MOBA_KV_HEADS, MOBA_GROUP, HEAD_DIM).transpose(0, 2, 1, 3).reshape(d, nq)
    w = {"g_q_lora": g_q_lora, "g_kv_lora": g_kv_lora}
    w["w_in_p"] = w_in_layout(mq)
    w["w_in_s"] = w_in_layout(mq_gkd)
    w["gattn"] = row(g_attn)
    w["gql"] = row(g_q_lora)
    w["gkvl"] = row(g_kv_lora)
    w["gmq"] = row(jnp.tile(g_moba_q, MOBA_HEADS)) * sm
    w["gmqs"] = row(jnp.tile(_swap_halves(g_moba_q, HEAD_DIM), MOBA_HEADS)) * sm
    w["gmk"] = row(jnp.tile(g_moba_k, MOBA_KV_HEADS))
    w["gmks"] = row(jnp.tile(_swap_halves(g_moba_k, HEAD_DIM), MOBA_KV_HEADS))
    w["gkr"] = row(_pad_lanes(g_k_rope, LANE))
    w["gkrs"] = row(_pad_lanes(_swap_halves(g_k_rope, MLA_ROPE), LANE))
    w["g512"] = _group_matrix(nq, [(0, HEAD_DIM)], HEAD_DIM)
    w["g128"] = _group_matrix(LANE, [(0, MLA_ROPE)], LANE)
    w["g1024"] = _group_matrix(MLA_HEADS * QPAD, [(0, MLA_NOPE), (MLA_NOPE, MLA_ROPE)], QPAD)
    tail = QPAD - MLA_NOPE - MLA_ROPE
    w["icq"] = row(jnp.tile(jnp.concatenate([jnp.full((MLA_NOPE,), 1.0 / MLA_NOPE), jnp.full((MLA_ROPE,), 1.0 / MLA_ROPE),
                                             jnp.ones((tail,))]), MLA_HEADS))
    uq = w_uq.reshape(ql, MLA_HEADS, MLA_NOPE + MLA_ROPE)
    nope, rope = uq[..., :MLA_NOPE], uq[..., MLA_NOPE:]
    zt = jnp.zeros((ql, MLA_HEADS, tail), F32)
    w["wuqa"] = jnp.concatenate([nope, rope, zt], axis=-1).reshape(ql, -1).astype(BF16)
    w["wuqb"] = jnp.concatenate([jnp.zeros_like(nope), _swap_halves(rope, MLA_ROPE), zt], axis=-1).reshape(ql, -1).astype(BF16)
    z1 = jnp.zeros((tail,), F32)
    w["gqa"] = row(jnp.tile(jnp.concatenate([g_q_nope, g_q_rope, z1]), MLA_HEADS)) * sq
    w["gqb"] = row(jnp.tile(jnp.concatenate([jnp.zeros((MLA_NOPE,), F32), _swap_halves(g_q_rope, MLA_ROPE), z1]),
                            MLA_HEADS)) * sq
    w["wuk_pad"] = _pad_lanes(w_uk, QPAD).reshape(kvl, -1).astype(BF16)
    w["gkn"] = row(jnp.tile(_pad_lanes(g_k_nope, QPAD), MLA_HEADS))
    j = jnp.arange(LANE)[:, None]
    c = jnp.arange(MLA_HEADS * QPAD)[None, :]
    w["ekr"] = ((j < MLA_ROPE) & (c % QPAD == MLA_NOPE + j)).astype(BF16)
    w["wuv"] = w_uv.reshape(kvl, -1).astype(BF16)
    w["wukt"] = w_uk.reshape(kvl, -1).T.astype(BF16)
    head_of_col = jnp.arange(MLA_HEADS * MLA_V) // MLA_V
    w["wuv_pad"] = jnp.where(head_of_col[None, None, :] == jnp.arange(MLA_HEADS)[:, None, None],
                             w_uv.reshape(1, kvl, -1), 0.0).astype(BF16)

    amax = lambda v: jnp.max(jnp.abs(v))
    w["mla_bound"] = BF16_SLACK * sq * (MLA_NOPE * amax(g_q_nope) * amax(g_k_nope) + MLA_ROPE * amax(g_q_rope) * amax(g_k_rope))
    w["moba_bound"] = BF16_SLACK * sm * HEAD_DIM * amax(g_moba_q) * amax(g_moba_k)

    w["wgate"] = jnp.concatenate([ga, gb], axis=1).astype(BF16)
    w["woa"] = w_oa.astype(BF16)
    w["wob_p"] = w_ob.astype(BF16)
    w["wob_s"] = w_ob.reshape(MOBA_KV_HEADS, MOBA_GROUP, HEAD_DIM, -1).transpose(1, 0, 2, 3).reshape(nq, -1).astype(BF16)
    w["wout"] = w_out.astype(BF16)
    w["gffn"] = row(g_ffn)
    wr = _pad_lanes(jnp.concatenate([w_expert_router, w_group], axis=1), LANE)
    w["wr_hi"], w["wr_lo"] = _split_bf16(wr)
    w["br"] = row(_pad_lanes(jnp.concatenate([b_expert_router, b_group]), LANE))
    w["wg"] = w_exp_gate.astype(BF16)
    w["wu"] = w_exp_up.astype(BF16)
    w["wd"] = w_exp_down.astype(BF16)
    return w


def _row_tile(t, want):
    return want if t % want == 0 else t


def kernel(x_prompt, x_sample, cache_ckv, cache_krope, cache_k, cache_v, page_table, g_attn, w_in, g_q_lora, g_kv_lora, w_uq, g_q_nope, g_q_rope, g_k_rope, w_uk, w_uv, g_k_nope, g_moba_q, g_moba_k, w_oa, w_ob, w_out, g_ffn, w_group, b_group, w_expert_router, b_expert_router, w_exp_gate, w_exp_up, w_exp_down):
    w = _prep_weights(g_attn, w_in, g_q_lora, g_kv_lora, w_uq, g_q_nope, g_q_rope, g_k_rope, w_uk, w_uv, g_k_nope,
                      g_moba_q, g_moba_k, w_oa, w_ob, w_out, g_ffn, w_group, b_group, w_expert_router,
                      b_expert_router, w_exp_gate, w_exp_up, w_exp_down)
    bp, t, d = x_prompt.shape
    bs, ts, _ = x_sample.shape
    npool, page = cache_ckv.shape[:2]
    past = page_table.shape[1] * page
    assert bp == 1 and t % MOBA_BLOCK == 0
    assert past % MOBA_BLOCK == 0 and ts <= LANE and ts % 8 == 0

    xp = x_prompt.reshape(t, d)
    ckv_p, kr_p, mk_p, mv_p, qt, kp, vt, mqt, mkb, mvt = _features(
        xp, jnp.arange(t, dtype=jnp.int32), w, True, _row_tile(t, 256))
    tq = _row_tile(t, 512)
    oat = lax.cond(w["mla_bound"] <= EXP2_SAFE, functools.partial(_mla_prompt, False, tq, tq, ATTN_TILE_GROUP),
                   functools.partial(_mla_prompt, True, tq, tq, 1), qt, kp, vt)
    obt = lax.cond(w["moba_bound"] <= EXP2_SAFE, functools.partial(_moba_prompt, False, tq, ATTN_TILE_GROUP),
                   functools.partial(_moba_prompt, True, tq, 1), mqt, mkb, mvt, _block_means(mk_p))
    x1, hn, gates = _merge(xp, oat, obt, w, True, _row_tile(t, 256))
    y_p = _moe(x1, hn, gates, w, _row_tile(t, 1024))

    xs = x_sample.reshape(bs * ts, d)
    pos_s = past + jnp.tile(jnp.arange(ts, dtype=jnp.int32), bs)
    ckv_s, kr_s, mk_s, mv_s, q_s, mq_s = _features(xs, pos_s, w, False, _row_tile(bs * ts, 256))
    kvl = ckv_s.shape[1]
    nkv = MOBA_KV_HEADS * HEAD_DIM
    oa_s = _mla_sample(page_table, q_s.reshape(bs, ts, -1), ckv_s.reshape(bs, ts, kvl), kr_s.reshape(bs, ts, -1),
                       cache_ckv, jnp.transpose(cache_krope, (0, 2, 1)), w)
    ob_s = _moba_sample(page_table, mq_s.reshape(bs, ts, -1), mk_s.reshape(bs, ts, nkv), mv_s.reshape(bs, ts, nkv),
                        jnp.transpose(cache_k, (0, 2, 3, 1)).reshape(npool, nkv, page),
                        jnp.transpose(cache_v, (0, 2, 3, 1)).reshape(npool, nkv, page))
    x1s, hns, gates_s = _merge(xs, oa_s.reshape(bs * ts, -1), ob_s.reshape(bs * ts, -1), w, False,
                               _row_tile(bs * ts, 256))
    y_s = _moe(x1s, hns, gates_s, w, _row_tile(bs * ts, 1024))

    return (y_p.reshape(bp, t, d), y_s.reshape(bs, ts, d),
            ckv_p.reshape(bp, t, kvl), kr_p.reshape(bp, t, MLA_ROPE),
            mk_p.reshape(bp, t, MOBA_KV_HEADS, HEAD_DIM), mv_p.reshape(bp, t, MOBA_KV_HEADS, HEAD_DIM),
            ckv_s.reshape(bs, ts, kvl), kr_s.reshape(bs, ts, MLA_ROPE),
            mk_s.reshape(bs, ts, MOBA_KV_HEADS, HEAD_DIM), mv_s.reshape(bs, ts, MOBA_KV_HEADS, HEAD_DIM))
```

```python
import functools
import math

import jax
import jax.numpy as jnp
from jax import lax
from jax.experimental import pallas as pl
from jax.experimental.pallas import tpu as pltpu

MLA_HEADS = 8
MLA_NOPE = 64
MLA_ROPE = 32
MLA_V = 64
MOBA_HEADS = 8
MOBA_KV_HEADS = 4
MOBA_GROUP = MOBA_HEADS // MOBA_KV_HEADS
HEAD_DIM = 64
MOBA_BLOCK = 256
MOBA_TOPK = 3
N_GROUPS = 4
EXPERTS_PER_GROUP = 8
N_EXPERTS = N_GROUPS * EXPERTS_PER_GROUP
ROPE_THETA = 10000.0
NORM_EPS = 1e-6

LANE = 128
QPAD = LANE
LOG2E = 1.4426950408889634
NEG = -1e30
EXP2_SAFE = 40.0
BF16_SLACK = 1.02
ATTN_TILE_GROUP = 2
MLA_SAMPLE_PAGES_PER_CHUNK = 16
MOBA_SAMPLE_PAGES_PER_CHUNK = 8
MLA_SAMPLE_SLOTS = 3
MOBA_SAMPLE_SLOTS = 4
ROW_TILE = 512
GROUP_SUM_PERIOD = 256
MOE_PACKED_ROWS = 320
VMEM_LIMIT = 56 * 1024 * 1024

F32 = jnp.float32
BF16 = jnp.bfloat16


def _dot(a, b):
    return jnp.dot(a, b, preferred_element_type=F32)


def _dot_nt(a, b):
    return lax.dot_general(a, b, (((1,), (1,)), ((), ())), preferred_element_type=F32)


def _split_bf16(x):
    hi = x.astype(BF16)
    lo = (x - hi.astype(F32)).astype(BF16)
    return hi, lo


def _group_ssq(x, g):
    hi, lo = _split_bf16(x * x)
    w = g.shape[0]
    parts = [_dot(hi[:, o:o + w], g) + _dot(lo[:, o:o + w], g) for o in range(0, x.shape[1], w)]
    return parts[0] if len(parts) == 1 else jnp.concatenate(parts, axis=1)


def _rms(x):
    return x * lax.rsqrt(jnp.mean(x * x, axis=-1, keepdims=True) + NORM_EPS)


def _tile_lanes(x, n):
    return x if n == 1 else jnp.concatenate([x] * n, axis=1)


def _feat_kernel(prompt, dims, *refs):
    ql, kvl = dims
    (x_ref, gattn, w_in, gql, gkvl, gmq, gmqs, gmk, gmks, gkr, gkrs,
     c64, s64, c32, s32, cqt, sqt, g512, g128, g1024, icq, wuqa, wuqb, gqa, gqb) = refs[:25]
    rest = refs[25:]
    x = x_ref[...]
    h = (_rms(x) * gattn[...]).astype(BF16)
    z = _dot(h, w_in[...])
    o = 0
    cq = z[:, o:o + ql]; o += ql
    ckv = z[:, o:o + kvl]; o += kvl
    mq = z[:, o:o + 512]; o += 512
    mqs = z[:, o:o + 512]; o += 512
    mk = z[:, o:o + 256]; o += 256
    mks = z[:, o:o + 256]; o += 256
    mv = z[:, o:o + 256]; o += 256
    kr = z[:, o:o + 128]; o += 128
    krs = z[:, o:o + 128]; o += 128

    cqn = _rms(cq) * gql[...]
    ckvn = _rms(ckv) * gkvl[...]

    r = lax.rsqrt(_group_ssq(kr, g128[...]) * (1.0 / MLA_ROPE) + NORM_EPS)
    krn = r * (kr * gkr[...] * c32[...] + krs * gkrs[...] * s32[...])

    c2 = _tile_lanes(c64[...], 2)
    s2 = _tile_lanes(s64[...], 2)
    r = lax.rsqrt(_group_ssq(mk, g512[...]) * (1.0 / HEAD_DIM) + NORM_EPS)
    mkn = r * (mk * gmk[...] * c2 + mks * gmks[...] * s2)

    c4 = _tile_lanes(c2, 2)
    s4 = _tile_lanes(s2, 2)
    r = lax.rsqrt(_group_ssq(mq, g512[...]) * (1.0 / HEAD_DIM) + NORM_EPS)
    mqn = r * (mq * gmq[...] * c4 + mqs * gmqs[...] * s4)

    cqb = cqn.astype(BF16)
    qa = _dot(cqb, wuqa[...])
    qb = _dot(cqb, wuqb[...])
    r = lax.rsqrt(_group_ssq(qa, g1024[...]) * icq[...] + NORM_EPS)
    c8 = _tile_lanes(cqt[...], MLA_HEADS)
    s8 = _tile_lanes(sqt[...], MLA_HEADS)
    qn = r * (qa * gqa[...] * c8 + qb * gqb[...] * s8)

    if prompt:
        wuk, gkn, ekr, wuv = rest[:4]
        ckv_o, kr_o, mk_o, mv_o, qt_o, kp_o, vt_o, mqt_o, mkb_o, mvt_o = rest[4:]
    else:
        ckv_o, kr_o, mk_o, mv_o, q_o, mq_o = rest
    ckv_o[...] = ckvn
    kr_o[...] = krn[:, :MLA_ROPE]
    mk_o[...] = mkn
    mv_o[...] = mv
    if prompt:
        ckvb = ckvn.astype(BF16)
        kraw = _dot(ckvb, wuk[...])
        rk = lax.rsqrt(_group_ssq(kraw, g1024[...]) * icq[...] + NORM_EPS)
        kp = rk * kraw * gkn[...] + _dot(krn.astype(BF16), ekr[...])
        kp_o[...] = kp.astype(BF16)
        qt_o[...] = qn.T.astype(BF16)
        vt_o[...] = _dot(ckvb, wuv[...]).T.astype(BF16)
        mqt_o[...] = mqn.T.astype(BF16)
        mkb_o[...] = mkn.astype(BF16)
        mvt_o[...] = mv.T.astype(BF16)
    else:
        q_o[...] = qn
        mq_o[...] = mqn


def _swap_halves(w, hd):
    lead = w.shape[:-1]
    n = w.shape[-1] // hd
    w = w.reshape(lead + (n, 2, hd // 2))
    return jnp.flip(w, axis=-2).reshape(lead + (n * hd,))


def _pad_lanes(w, width):
    return jnp.pad(w, [(0, 0)] * (w.ndim - 1) + [(0, width - w.shape[-1])])


def _group_matrix(width, starts_sizes, period):
    lane = jnp.arange(width)
    blk = lane // period
    off = lane % period
    gid = jnp.full((width,), -1, jnp.int32)
    for k, (st, sz) in enumerate(starts_sizes):
        gid = jnp.where((off >= st) & (off < st + sz), k, gid)
    same = (blk[:, None] == blk[None, :]) & (gid[:, None] == gid[None, :]) & (gid[:, None] >= 0)
    return same.astype(BF16)


def _rope_tables(pos, half):
    inv = ROPE_THETA ** (-jnp.arange(half, dtype=F32) / half)
    ang = pos.astype(F32)[:, None] * inv[None, :]
    return jnp.cos(ang), jnp.sin(ang)


def _features(x2d, pos, w, prompt, tm):
    t, d = x2d.shape
    ql = w["g_q_lora"].shape[0]
    kvl = w["g_kv_lora"].shape[0]
    assert t % tm == 0
    cos32, sin32 = _rope_tables(pos, HEAD_DIM // 2)
    cos16, sin16 = _rope_tables(pos, MLA_ROPE // 2)
    c64 = jnp.concatenate([cos32, cos32] * 2, axis=1)
    s64 = jnp.concatenate([-sin32, sin32] * 2, axis=1)
    z96 = jnp.zeros((t, LANE - MLA_ROPE), F32)
    c32 = jnp.concatenate([cos16, cos16, z96], axis=1)
    s32 = jnp.concatenate([-sin16, sin16, z96], axis=1)
    z32 = jnp.zeros((t, QPAD - MLA_NOPE - MLA_ROPE), F32)
    cqt = jnp.concatenate([jnp.ones((t, MLA_NOPE), F32), cos16, cos16, z32], axis=1)
    sqt = jnp.concatenate([jnp.zeros((t, MLA_NOPE), F32), -sin16, sin16, z32], axis=1)

    row_args = [x2d, w["gattn"], w["w_in_p" if prompt else "w_in_s"], w["gql"], w["gkvl"],
                w["gmq"], w["gmqs"], w["gmk"], w["gmks"], w["gkr"], w["gkrs"]]
    tab_args = [c64, s64, c32, s32, cqt, sqt]
    mat_args = [w["g512"], w["g128"], w["g1024"], w["icq"], w["wuqa"], w["wuqb"], w["gqa"], w["gqb"]]
    if prompt:
        mat_args += [w["wuk_pad"], w["gkn"], w["ekr"], w["wuv"]]

    def full(a):
        return pl.BlockSpec(a.shape, lambda i: (0,) * a.ndim)

    in_specs = [pl.BlockSpec((tm, d), lambda i: (i, 0))] + [full(a) for a in row_args[1:]]
    in_specs += [pl.BlockSpec((tm, LANE), lambda i: (i, 0)) for _ in tab_args]
    in_specs += [full(a) for a in mat_args]

    def rows(width, dtype):
        return jax.ShapeDtypeStruct((t, width), dtype), pl.BlockSpec((tm, width), lambda i: (i, 0))

    def cols(height, dtype):
        return jax.ShapeDtypeStruct((height, t), dtype), pl.BlockSpec((height, tm), lambda i: (0, i))

    outs = [rows(kvl, F32), rows(MLA_ROPE, F32), rows(256, F32), rows(256, F32)]
    if prompt:
        outs += [cols(MLA_HEADS * QPAD, BF16), rows(MLA_HEADS * QPAD, BF16), cols(MLA_HEADS * MLA_V, BF16),
                 cols(512, BF16), rows(256, BF16), cols(256, BF16)]
    else:
        outs += [rows(MLA_HEADS * QPAD, F32), rows(512, F32)]
    return pl.pallas_call(
        functools.partial(_feat_kernel, prompt, (ql, kvl)),
        grid=(t // tm,),
        in_specs=in_specs,
        out_specs=[o[1] for o in outs],
        out_shape=[o[0] for o in outs],
        compiler_params=pltpu.CompilerParams(dimension_semantics=("parallel",), vmem_limit_bytes=VMEM_LIMIT),
        name="features_prompt" if prompt else "features_sample",
    )(*row_args, *tab_args, *mat_args)


def _softmax_init(online, dv, tq):
    if online:
        return (jnp.full((1, tq), NEG, F32), jnp.zeros((1, tq), F32), jnp.zeros((dv, tq), F32))
    return (jnp.zeros((8, tq), F32), jnp.zeros((dv, tq), F32))


def _softmax_step(online, s, vt, carry):
    if online:
        m, l, acc = carry
        m_new = jnp.maximum(m, jnp.max(s, axis=0, keepdims=True))
        alpha = jnp.exp2(m - m_new)
        p = jnp.exp2(s - m_new)
        l = alpha * l + jnp.sum(p, axis=0, keepdims=True)
        acc = alpha * acc + _dot(vt, p.astype(BF16))
        return m_new, l, acc
    l8, acc = carry
    p = jnp.exp2(s)
    l8 = l8 + jnp.sum(p.reshape(s.shape[0] // 8, 8, s.shape[1]), axis=0)
    acc = acc + _dot(vt, p.astype(BF16))
    return l8, acc


def _softmax_finish(online, carry):
    if online:
        _, l, acc = carry
    else:
        l8, acc = carry
        l = jnp.sum(l8, axis=0, keepdims=True)
    return acc / l


def _key_tile_loops(i, tq, tk, group, tile, carry):
    big = group * tk
    n_big = (i * tq) // big
    carry = lax.fori_loop(0, n_big, lambda j, c: tile(pl.multiple_of(j * big, big), big, c, False), carry)
    carry = lax.fori_loop(n_big * group, (i * tq) // tk,
                          lambda j, c: tile(pl.multiple_of(j * tk, tk), tk, c, False), carry)
    for u in range(tq // tk):
        carry = tile(pl.multiple_of(i * tq + u * tk, tk), tk, carry, True)
    return carry


def _mla_prompt_kernel(online, tq, tk, group, qt_ref, kp_ref, vt_ref, o_ref):
    i = pl.program_id(1)
    q = qt_ref[...]

    def tile(off, size, carry, masked):
        s = _dot(kp_ref[pl.ds(off, size), :], q)
        if masked:
            kpos = off + lax.broadcasted_iota(jnp.int32, (size, tq), 0)
            qpos = i * tq + lax.broadcasted_iota(jnp.int32, (size, tq), 1)
            s = jnp.where(kpos <= qpos, s, NEG)
        return _softmax_step(online, s, vt_ref[:, pl.ds(off, size)], carry)

    carry = _key_tile_loops(i, tq, tk, group, tile, _softmax_init(online, vt_ref.shape[0], tq))
    o_ref[...] = _softmax_finish(online, carry)


def _mla_prompt(online, tq, tk, group, qt, kp, vt):
    t = kp.shape[0]
    nh = MLA_HEADS
    return pl.pallas_call(
        functools.partial(_mla_prompt_kernel, online, tq, tk, group),
        grid=(nh, t // tq),
        in_specs=[pl.BlockSpec((QPAD, tq), lambda h, i: (h, i)),
                  pl.BlockSpec((t, QPAD), lambda h, i: (0, h)),
                  pl.BlockSpec((MLA_V, t), lambda h, i: (h, 0))],
        out_specs=pl.BlockSpec((MLA_V, tq), lambda h, i: (h, i)),
        out_shape=jax.ShapeDtypeStruct((nh * MLA_V, t), F32),
        compiler_params=pltpu.CompilerParams(dimension_semantics=("parallel", "arbitrary"),
                                             vmem_limit_bytes=VMEM_LIMIT),
        name="mla_prompt_online" if online else "mla_prompt",
    )(qt, kp, vt)


def _block_mean_kernel(k_ref, o_ref):
    k = k_ref[...]
    n = o_ref.shape[0]
    o_ref[...] = jnp.mean(k.reshape(n, MOBA_BLOCK, k.shape[1]), axis=1)


def _block_means(k2d):
    t, wdt = k2d.shape
    nb = t // MOBA_BLOCK
    per = 8 if nb % 8 == 0 else nb
    return pl.pallas_call(
        _block_mean_kernel,
        grid=(nb // per,),
        in_specs=[pl.BlockSpec((per * MOBA_BLOCK, wdt), lambda i: (i, 0))],
        out_specs=pl.BlockSpec((per, wdt), lambda i: (i, 0)),
        out_shape=jax.ShapeDtypeStruct((nb, wdt), F32),
        compiler_params=pltpu.CompilerParams(dimension_semantics=("parallel",)),
        name="moba_block_means",
    )(k2d)


def _select_topk_rows(s, limit):
    nb = s.shape[0]
    row = lax.broadcasted_iota(jnp.int32, s.shape, 0)
    s = jnp.where(row < limit, s, NEG)
    sel = jnp.zeros(s.shape, F32)
    for _ in range(MOBA_TOPK):
        mx = jnp.max(s, axis=0, keepdims=True)
        first = jnp.min(jnp.where(s == mx, row, nb), axis=0, keepdims=True)
        hit = (row == first) & (mx > 0.5 * NEG)
        sel = jnp.where(hit, 1.0, sel)
        s = jnp.where(row == first, NEG, s)
    return sel


def _moba_prompt_kernel(online, tq, group, q_ref, k_ref, vt_ref, mean_ref, o_ref, sel_ref):
    h = pl.program_id(0)
    i = pl.program_id(1)
    tk = MOBA_BLOCK
    odd = (h // MOBA_GROUP) % 2
    q64 = q_ref[...]
    zero = jnp.zeros_like(q64)
    q = jnp.concatenate([jnp.where(odd == 0, q64, zero), jnp.where(odd == 1, q64, zero)], axis=0)
    qpos1 = i * tq + lax.broadcasted_iota(jnp.int32, (1, tq), 1)
    cur1 = qpos1 // MOBA_BLOCK
    mhi, mlo = _split_bf16(mean_ref[...])
    sel_ref[...] = _select_topk_rows(_dot(mhi, q) + _dot(mlo, q), cur1)

    def tile(off, size, carry, own):
        s = _dot(k_ref[pl.ds(off, size), :], q)
        n0 = off // tk
        slabs = []
        for u in range(size // tk):
            su = s[u * tk:(u + 1) * tk]
            picked = jnp.where(sel_ref[pl.ds(n0 + u, 1), :] > 0.0, su, NEG)
            if own:
                kpos = off + u * tk + lax.broadcasted_iota(jnp.int32, (tk, tq), 0)
                qpos = i * tq + lax.broadcasted_iota(jnp.int32, (tk, tq), 1)
                picked = jnp.where(n0 + u == cur1, jnp.where(kpos <= qpos, su, NEG), picked)
            slabs.append(picked)
        s = slabs[0] if len(slabs) == 1 else jnp.concatenate(slabs, axis=0)
        return _softmax_step(online, s, vt_ref[:, pl.ds(off, size)], carry)

    carry = _key_tile_loops(i, tq, tq, group, tile, _softmax_init(online, vt_ref.shape[0], tq))
    o_ref[...] = _softmax_finish(online, carry)


def _moba_prompt(online, tq, group, mqt, mkb, mvt, means):
    t = mkb.shape[0]
    nb = means.shape[0]
    g = MOBA_GROUP
    return pl.pallas_call(
        functools.partial(_moba_prompt_kernel, online, tq, group),
        grid=(MOBA_HEADS, t // tq),
        in_specs=[pl.BlockSpec((HEAD_DIM, tq), lambda h, i: (h, i)),
                  pl.BlockSpec((t, LANE), lambda h, i: (0, h // (2 * g))),
                  pl.BlockSpec((HEAD_DIM, t), lambda h, i: (h // g, 0)),
                  pl.BlockSpec((nb, LANE), lambda h, i: (0, h // (2 * g)))],
        out_specs=pl.BlockSpec((HEAD_DIM, tq), lambda h, i: (h, i)),
        out_shape=jax.ShapeDtypeStruct((MOBA_HEADS * HEAD_DIM, t), F32),
        scratch_shapes=[pltpu.VMEM((nb, tq), F32)],
        compiler_params=pltpu.CompilerParams(dimension_semantics=("parallel", "arbitrary"),
                                             vmem_limit_bytes=VMEM_LIMIT),
        name="moba_prompt_online" if online else "moba_prompt",
    )(mqt, mkb, mvt, means)


def _merge_kernel(transposed, x_ref, oa_ref, ob_ref, gattn, wgate, woa, wob, wout, gffn, wr_hi, wr_lo, br,
                  x1_o, hn_o, gates_o):
    x = x_ref[...]
    d = x.shape[1]
    h = (_rms(x) * gattn[...]).astype(BF16)
    gate = jax.nn.sigmoid(_dot(h, wgate[...]))
    oa = oa_ref[...]
    ob = ob_ref[...]
    if transposed:
        oa = oa.T
        ob = ob.T
    mix = gate[:, :d] * _dot(oa.astype(BF16), woa[...]) + gate[:, d:] * _dot(ob.astype(BF16), wob[...])
    x1 = x + _dot(mix.astype(BF16), wout[...])
    x1_o[...] = x1
    hn = _rms(x1) * gffn[...]
    hn_o[...] = hn.astype(BF16)

    hhi, hlo = _split_bf16(hn)
    lg = _dot(hhi, wr_hi[...]) + _dot(hlo, wr_hi[...]) + _dot(hhi, wr_lo[...]) + br[...]
    lane = lax.broadcasted_iota(jnp.int32, lg.shape, 1)
    is_g = (lane >= N_EXPERTS) & (lane < N_EXPERTS + N_GROUPS)
    gl = jnp.where(is_g, lg, NEG)
    gmax = jnp.max(gl, axis=-1, keepdims=True)
    gsum = jnp.sum(jnp.where(is_g, jnp.exp(gl - gmax), 0.0), axis=-1, keepdims=True)
    p_g = 1.0 / gsum
    g_idx = jnp.min(jnp.where(gl == gmax, lane - N_EXPERTS, LANE), axis=-1, keepdims=True)
    in_grp = (lane < N_EXPERTS) & (lane // EXPERTS_PER_GROUP == g_idx)
    el = jnp.where(in_grp, lg, NEG)
    v1 = jnp.max(el, axis=-1, keepdims=True)
    i1 = jnp.min(jnp.where(el == v1, lane, LANE), axis=-1, keepdims=True)
    el2 = jnp.where(lane == i1, NEG, el)
    v2 = jnp.max(el2, axis=-1, keepdims=True)
    i2 = jnp.min(jnp.where(el2 == v2, lane, LANE), axis=-1, keepdims=True)
    e2 = jnp.exp(v2 - v1)
    w1 = p_g / (1.0 + e2)
    w2 = p_g * e2 / (1.0 + e2)
    gates_o[...] = jnp.where(lane == i1, w1, 0.0) + jnp.where(lane == i2, w2, 0.0)


def _merge(x2d, oa, ob, w, transposed, tm):
    t, d = x2d.shape
    da = MLA_HEADS * MLA_V
    db = MOBA_HEADS * HEAD_DIM
    wargs = [w["gattn"], w["wgate"], w["woa"], w["wob_p" if transposed else "wob_s"], w["wout"], w["gffn"],
             w["wr_hi"], w["wr_lo"], w["br"]]

    def full(a):
        return pl.BlockSpec(a.shape, lambda i: (0,) * a.ndim)

    if transposed:
        o_specs = [pl.BlockSpec((da, tm), lambda i: (0, i)), pl.BlockSpec((db, tm), lambda i: (0, i))]
    else:
        o_specs = [pl.BlockSpec((tm, da), lambda i: (i, 0)), pl.BlockSpec((tm, db), lambda i: (i, 0))]
    return pl.pallas_call(
        functools.partial(_merge_kernel, transposed),
        grid=(t // tm,),
        in_specs=[pl.BlockSpec((tm, d), lambda i: (i, 0))] + o_specs + [full(a) for a in wargs],
        out_specs=[pl.BlockSpec((tm, d), lambda i: (i, 0)), pl.BlockSpec((tm, d), lambda i: (i, 0)),
                   pl.BlockSpec((tm, LANE), lambda i: (i, 0))],
        out_shape=[jax.ShapeDtypeStruct((t, d), F32), jax.ShapeDtypeStruct((t, d), BF16),
                   jax.ShapeDtypeStruct((t, LANE), F32)],
        compiler_params=pltpu.CompilerParams(dimension_semantics=("parallel",), vmem_limit_bytes=VMEM_LIMIT),
        name="merge_prompt" if transposed else "merge_sample",
    )(x2d, oa, ob, *wargs)


def _expert_ffn(x, gcol, wg_ref, wu_ref, wd_ref):
    a = _dot(x, wg_ref[0])
    hid = a * jax.nn.sigmoid(a) * _dot(x, wu_ref[0])
    return _dot((hid * gcol).astype(BF16), wd_ref[0])


def _moe_kernel(fast, x1_ref, hn_ref, gates_ref, ltri_ref, wg_ref, wu_ref, wd_ref, o_ref,
                xg, gg, yg, scat, cnt_s):
    g = pl.program_id(1)
    e = pl.program_id(2)
    tm = hn_ref.shape[0]
    lane = lax.broadcasted_iota(jnp.int32, (tm, LANE), 1)

    @pl.when((g == 0) & (e == 0))
    def _():
        o_ref[...] = x1_ref[...]

    @pl.when(e == 0)
    def _():
        gates = gates_ref[...]
        in_grp = jnp.sum(jnp.where(lane // EXPERTS_PER_GROUP == g, gates, 0.0), axis=-1, keepdims=True) > 0.0
        ind = jnp.where(in_grp, 1.0, 0.0) + jnp.zeros((tm, LANE), F32)
        ltri = ltri_ref[...]
        rank_col = _dot(ltri, ind.astype(BF16))
        rank_row = _dot_nt(ind.T.astype(BF16), ltri)
        cnt_s[0] = jnp.max(rank_col).astype(jnp.int32)
        slot_row = jnp.where(ind.T[0:1, :] > 0.5, rank_row[0:1, :] - 1.0, -1.0)
        slot_col = jnp.where(ind[:, 0:1] > 0.5, rank_col[:, 0:1] - 1.0, -1.0)
        r_id = lax.broadcasted_iota(jnp.int32, (fast, tm), 0).astype(F32)
        pack = jnp.where(r_id == slot_row, 1.0, 0.0).astype(BF16)
        c_id = lax.broadcasted_iota(jnp.int32, (tm, fast), 1).astype(F32)
        scat[...] = jnp.where(c_id == slot_col, 1.0, 0.0).astype(BF16)
        xg[...] = _dot(pack, hn_ref[...]).astype(BF16)
        ghi, glo = _split_bf16(gates)
        gg[...] = _dot(pack, ghi) + _dot(pack, glo)
        yg[...] = jnp.zeros_like(yg)

    eg = g * EXPERTS_PER_GROUP + e
    packed = cnt_s[0] <= fast

    @pl.when(packed)
    def _():
        lane_p = lax.broadcasted_iota(jnp.int32, (fast, LANE), 1)
        gcol = jnp.sum(jnp.where(lane_p == eg, gg[...], 0.0), axis=-1, keepdims=True)
        yg[...] += _expert_ffn(xg[...], gcol, wg_ref, wu_ref, wd_ref)

    @pl.when(packed & (e == EXPERTS_PER_GROUP - 1))
    def _():
        o_ref[...] += _dot(scat[...], yg[...].astype(BF16))

    @pl.when(jnp.logical_not(packed))
    def _():
        gcol = jnp.sum(jnp.where(lane == eg, gates_ref[...], 0.0), axis=-1, keepdims=True)
        o_ref[...] += _expert_ffn(hn_ref[...], gcol, wg_ref, wu_ref, wd_ref)


def _moe(x1, hn, gates, w, tm):
    t, d = x1.shape
    ne, _, f = w["wg"].shape
    fast = min(MOE_PACKED_ROWS, tm)
    ltri = (jnp.arange(tm)[:, None] >= jnp.arange(tm)[None, :]).astype(BF16)
    wmap = lambda i, g, e: (g * EXPERTS_PER_GROUP + e, 0, 0)
    return pl.pallas_call(
        functools.partial(_moe_kernel, fast),
        grid=(t // tm, N_GROUPS, EXPERTS_PER_GROUP),
        in_specs=[pl.BlockSpec((tm, d), lambda i, g, e: (i, 0)),
                  pl.BlockSpec((tm, d), lambda i, g, e: (i, 0)),
                  pl.BlockSpec((tm, LANE), lambda i, g, e: (i, 0)),
                  pl.BlockSpec((tm, tm), lambda i, g, e: (0, 0)),
                  pl.BlockSpec((1, d, f), wmap),
                  pl.BlockSpec((1, d, f), wmap),
                  pl.BlockSpec((1, f, d), wmap)],
        out_specs=pl.BlockSpec((tm, d), lambda i, g, e: (i, 0)),
        out_shape=jax.ShapeDtypeStruct((t, d), F32),
        scratch_shapes=[pltpu.VMEM((fast, d), BF16), pltpu.VMEM((fast, LANE), F32), pltpu.VMEM((fast, d), F32),
                        pltpu.VMEM((tm, fast), BF16), pltpu.SMEM((1,), jnp.int32)],
        compiler_params=pltpu.CompilerParams(dimension_semantics=("parallel", "arbitrary", "arbitrary"),
                                             vmem_limit_bytes=VMEM_LIMIT),
        name="moe_experts",
    )(x1, hn, gates, ltri, w["wg"], w["wu"], w["wd"])


def _page_copies(pt_ref, b, chunk, slot, pages_per_chunk, srcs, bufs, sems):
    out = []
    for j in range(pages_per_chunk):
        pg = pt_ref[b, chunk * pages_per_chunk + j]
        for src, buf, sem in zip(srcs, bufs, sems):
            out.append(pltpu.make_async_copy(src.at[pg], buf.at[slot, j], sem.at[slot]))
    return out


def _stream_prime(total, nslot, nchunk, copies):
    for a in range(nslot - 1):
        @pl.when(a < total)
        def _():
            for cpy in copies(a // nchunk, a % nchunk, a % nslot):
                cpy.start()


def _stream_advance(g, total, nslot, nchunk, copies):
    ahead = g + (nslot - 1)

    @pl.when(ahead < total)
    def _():
        for cpy in copies(ahead // nchunk, ahead % nchunk, ahead % nslot):
            cpy.start()

    for cpy in copies(g // nchunk, g % nchunk, g % nslot):
        cpy.wait()
    return g % nslot


def _lane_softmax_step(s, v, m, l, acc):
    m_new = jnp.maximum(m, jnp.max(s, axis=-1, keepdims=True))
    alpha = jnp.exp2(m - m_new)
    p = jnp.exp2(s - m_new)
    l = alpha * l + jnp.sum(p, axis=-1, keepdims=True)
    acc = alpha * acc + _dot(p.astype(BF16), v)
    return m_new, l, acc


def _mla_sample_kernel(cp, pt_ref, q_ref, cnew_ref, rnew_ref, cckv, ckr, wukt, gkn, wuvp, o_ref,
                       ckv_buf, kr_buf, lhs, ckv_sem, kr_sem):
    b = pl.program_id(0)
    npages = pt_ref.shape[1]
    nchunk = npages // cp
    ts = q_ref.shape[1]
    nh = MLA_HEADS
    nrow = nh * ts
    page = ckv_buf.shape[2]
    tk = cp * page
    kvl = ckv_buf.shape[3]
    hk = nh * MLA_NOPE

    def copies(seq, c, slot):
        return _page_copies(pt_ref, seq, c, slot, cp, (cckv, ckr), (ckv_buf, kr_buf), (ckv_sem, kr_sem))

    nslot = ckv_buf.shape[0]
    total = pl.num_programs(0) * nchunk

    @pl.when(b == 0)
    def _():
        _stream_prime(total, nslot, nchunk, copies)

    q = q_ref[0]
    qg = (q * gkn[...]).astype(BF16)
    qroll = pltpu.roll(q, MLA_HEADS * QPAD - MLA_NOPE, axis=1)
    lhs[0:hk, :] = wukt[...]
    qr_rows = []
    for h in range(nh):
        qh = qg[:, h * QPAD:h * QPAD + MLA_NOPE]
        lhs[hk + h * ts:hk + (h + 1) * ts, :] = _dot(qh, wukt[h * MLA_NOPE:(h + 1) * MLA_NOPE, :]).astype(BF16)
        qr_rows.append(qroll[:, h * QPAD:h * QPAD + MLA_ROPE])
    qr = jnp.concatenate(qr_rows, axis=0).astype(BF16)

    def scores(ckv_c, rope_scores):
        n = ckv_c.shape[0]
        r_all = _dot_nt(lhs[...], ckv_c)
        kn = r_all[:hk].reshape(nh, MLA_NOPE, n)
        rinv = lax.rsqrt(jnp.sum(kn * kn, axis=1) * (1.0 / MLA_NOPE) + NORM_EPS)
        s_abs = r_all[hk:].reshape(nh, ts, n) * rinv[:, None, :]
        return s_abs.reshape(nrow, n) + rope_scores

    def body(c, carry):
        slot = _stream_advance(b * nchunk + c, total, nslot, nchunk, copies)
        ckv_c = ckv_buf[slot].reshape(tk, kvl).astype(BF16)
        krt_c = jnp.concatenate([kr_buf[slot, j] for j in range(cp)], axis=1).astype(BF16)
        return _lane_softmax_step(scores(ckv_c, _dot(qr, krt_c)), ckv_c, *carry)

    init = (jnp.full((nrow, 1), NEG, F32), jnp.zeros((nrow, 1), F32), jnp.zeros((nrow, kvl), F32))
    carry = lax.fori_loop(0, nchunk, body, init)

    pad = LANE - ts
    cnew = jnp.concatenate([cnew_ref[0], jnp.zeros((pad, kvl), F32)], axis=0).astype(BF16)
    rnew = jnp.concatenate([rnew_ref[0], jnp.zeros((pad, MLA_ROPE), F32)], axis=0).astype(BF16)
    s = scores(cnew, _dot_nt(qr, rnew))
    key = lax.broadcasted_iota(jnp.int32, s.shape, 1)
    tok = lax.broadcasted_iota(jnp.int32, s.shape, 0) % ts
    s = jnp.where(key <= tok, s, NEG)
    _, l, acc = _lane_softmax_step(s, cnew, *carry)
    lat = (acc / l).astype(BF16)
    out = jnp.zeros((ts, nh * MLA_V), F32)
    for h in range(nh):
        out = out + _dot(lat[h * ts:(h + 1) * ts, :], wuvp[h])
    o_ref[0] = out


def _mla_sample(page_table, q_s, ckv_new, kr_new, cache_ckv, cache_krope_t, w):
    bsz, npages = page_table.shape
    ts = q_s.shape[1]
    page, kvl = cache_ckv.shape[1:]
    cp = math.gcd(npages, MLA_SAMPLE_PAGES_PER_CHUNK)
    nh = MLA_HEADS
    grid_spec = pltpu.PrefetchScalarGridSpec(
        num_scalar_prefetch=1,
        grid=(bsz,),
        in_specs=[pl.BlockSpec((1, ts, nh * QPAD), lambda b, pt: (b, 0, 0)),
                  pl.BlockSpec((1, ts, kvl), lambda b, pt: (b, 0, 0)),
                  pl.BlockSpec((1, ts, MLA_ROPE), lambda b, pt: (b, 0, 0)),
                  pl.BlockSpec(memory_space=pl.ANY),
                  pl.BlockSpec(memory_space=pl.ANY),
                  pl.BlockSpec(w["wukt"].shape, lambda b, pt: (0, 0)),
                  pl.BlockSpec(w["gkn"].shape, lambda b, pt: (0, 0)),
                  pl.BlockSpec(w["wuv_pad"].shape, lambda b, pt: (0, 0, 0))],
        out_specs=pl.BlockSpec((1, ts, nh * MLA_V), lambda b, pt: (b, 0, 0)),
        scratch_shapes=[pltpu.VMEM((MLA_SAMPLE_SLOTS, cp, page, kvl), F32),
                        pltpu.VMEM((MLA_SAMPLE_SLOTS, cp, MLA_ROPE, page), F32),
                        pltpu.VMEM((nh * MLA_NOPE + nh * ts, kvl), BF16),
                        pltpu.SemaphoreType.DMA((MLA_SAMPLE_SLOTS,)),
                        pltpu.SemaphoreType.DMA((MLA_SAMPLE_SLOTS,))],
    )
    return pl.pallas_call(
        functools.partial(_mla_sample_kernel, cp),
        grid_spec=grid_spec,
        out_shape=jax.ShapeDtypeStruct((bsz, ts, nh * MLA_V), F32),
        compiler_params=pltpu.CompilerParams(dimension_semantics=("arbitrary",), vmem_limit_bytes=VMEM_LIMIT),
        name="mla_sample",
    )(page_table, q_s, ckv_new, kr_new, cache_ckv, cache_krope_t, w["wukt"], w["gkn"], w["wuv_pad"])


def _moba_sample_kernel(cp, pt_ref, q_ref, knew_ref, vnew_ref, ckt, cvt, o_ref,
                        kbuf, vbuf, s_all, k_sem, v_sem):
    b = pl.program_id(0)
    npages = pt_ref.shape[1]
    nchunk = npages // cp
    ts = q_ref.shape[1]
    wdt, page = kbuf.shape[2:]
    tk = cp * page
    bpc = tk // MOBA_BLOCK
    nb = npages * page // MOBA_BLOCK
    nrow = MOBA_HEADS * ts

    def kcopies(seq, c, slot):
        return _page_copies(pt_ref, seq, c, slot, cp, (ckt,), (kbuf,), (k_sem,))

    def vcopies(seq, c, slot):
        return _page_copies(pt_ref, seq, c, slot, cp, (cvt,), (vbuf,), (v_sem,))

    nslot = kbuf.shape[0]
    total = pl.num_programs(0) * nchunk

    @pl.when(b == 0)
    def _():
        _stream_prime(total, nslot, nchunk, kcopies)
        _stream_prime(total, nslot, nchunk, vcopies)

    mq = q_ref[0]
    lane = lax.broadcasted_iota(jnp.int32, (ts, wdt), 1)
    rows = []
    for kvh in range(MOBA_KV_HEADS):
        for g in range(MOBA_GROUP):
            rows.append(jnp.where(lane // HEAD_DIM == kvh, mq[:, g * wdt:(g + 1) * wdt], 0.0))
    qhi, qlo = _split_bf16(jnp.concatenate(rows, axis=0))
    q2 = jnp.concatenate([qhi, qlo], axis=0)
    blk_lane = lax.broadcasted_iota(jnp.int32, (nrow, LANE), 1)

    def pass1(c, carry):
        gsum, gmax = carry
        slot = _stream_advance(b * nchunk + c, total, nslot, nchunk, kcopies)
        kt =jnp.concatenate([kbuf[slot, j] for j in range(cp)], axis=1).astype(BF16)
        s2 = _dot(q2, kt)
        s = s2[:nrow] + s2[nrow:]
        s_all[:, pl.ds(pl.multiple_of(c * tk, tk), tk)] = s
        for jb in range(bpc):
            blk = s[:, jb * MOBA_BLOCK:(jb + 1) * MOBA_BLOCK]
            here = blk_lane == c * bpc + jb
            gsum = jnp.where(here, jnp.sum(blk, axis=-1, keepdims=True), gsum)
            gmax = jnp.where(here, jnp.max(blk, axis=-1, keepdims=True), gmax)
        return gsum, gmax

    gsum, gmax = lax.fori_loop(0, nchunk, pass1, (jnp.full((nrow, LANE), NEG, F32), jnp.full((nrow, LANE), NEG, F32)))

    sg =jnp.where(blk_lane < nb, gsum, NEG)
    sel = jnp.zeros(sg.shape, F32)
    for _ in range(MOBA_TOPK):
        mx = jnp.max(sg, axis=-1, keepdims=True)
        first = jnp.min(jnp.where(sg == mx, blk_lane, LANE), axis=-1, keepdims=True)
        sel = jnp.where((blk_lane == first) & (mx > 0.5 * NEG), 1.0, sel)
        sg = jnp.where(blk_lane == first, NEG, sg)

    pad = LANE - ts
    knew = jnp.concatenate([knew_ref[0], jnp.zeros((pad, wdt), F32)], axis=0).astype(BF16)
    vnew = jnp.concatenate([vnew_ref[0], jnp.zeros((pad, wdt), F32)], axis=0).astype(BF16)
    s2 = _dot_nt(q2, knew)
    key = lax.broadcasted_iota(jnp.int32, (nrow, LANE), 1)
    tok = lax.broadcasted_iota(jnp.int32, (nrow, LANE), 0) % ts
    s_new = jnp.where(key <= tok, s2[:nrow] + s2[nrow:], NEG)
    m = jnp.maximum(jnp.max(jnp.where(sel > 0.5, gmax, NEG), axis=-1, keepdims=True),
                    jnp.max(s_new, axis=-1, keepdims=True))

    def pass2(c, carry):
        l, acc = carry
        slot = _stream_advance(b * nchunk + c, total, nslot, nchunk, vcopies)
        vt =jnp.concatenate([vbuf[slot, j] for j in range(cp)], axis=1).astype(BF16)
        s = s_all[:, pl.ds(pl.multiple_of(c * tk, tk), tk)]
        cols = []
        for jb in range(bpc):
            on = jnp.sum(jnp.where(blk_lane == c * bpc + jb, sel, 0.0), axis=-1, keepdims=True) > 0.5
            cols.append(jnp.exp2(jnp.where(on, s[:, jb * MOBA_BLOCK:(jb + 1) * MOBA_BLOCK] - m, NEG)))
        p = jnp.concatenate(cols, axis=1)
        return l + jnp.sum(p, axis=-1, keepdims=True), acc + _dot_nt(p.astype(BF16), vt)

    l, acc = lax.fori_loop(0, nchunk, pass2, (jnp.zeros((nrow, 1), F32), jnp.zeros((nrow, wdt), F32)))
    p = jnp.exp2(s_new - m)
    l = l + jnp.sum(p, axis=-1, keepdims=True)
    acc = acc + _dot(p.astype(BF16), vnew)
    o = acc / l
    outs = []
    for g in range(MOBA_GROUP):
        og = jnp.zeros((ts, wdt), F32)
        for kvh in range(MOBA_KV_HEADS):
            r0 = (kvh * MOBA_GROUP + g) * ts
            og = og + jnp.where(lane // HEAD_DIM == kvh, o[r0:r0 + ts, :], 0.0)
        outs.append(og)
    o_ref[0] = jnp.concatenate(outs, axis=1)


def _moba_sample(page_table, mq_s, mk_new, mv_new, cache_kt, cache_vt):
    bsz, npages = page_table.shape
    ts = mq_s.shape[1]
    wdt, page = cache_kt.shape[1:]
    cp = math.gcd(npages, MOBA_SAMPLE_PAGES_PER_CHUNK)
    past = npages * page
    nrow = MOBA_HEADS * ts
    assert (cp * page) % MOBA_BLOCK == 0 and past // MOBA_BLOCK <= LANE
    grid_spec = pltpu.PrefetchScalarGridSpec(
        num_scalar_prefetch=1,
        grid=(bsz,),
        in_specs=[pl.BlockSpec((1, ts, MOBA_HEADS * HEAD_DIM), lambda b, pt: (b, 0, 0)),
                  pl.BlockSpec((1, ts, wdt), lambda b, pt: (b, 0, 0)),
                  pl.BlockSpec((1, ts, wdt), lambda b, pt: (b, 0, 0)),
                  pl.BlockSpec(memory_space=pl.ANY),
                  pl.BlockSpec(memory_space=pl.ANY)],
        out_specs=pl.BlockSpec((1, ts, MOBA_HEADS * HEAD_DIM), lambda b, pt: (b, 0, 0)),
        scratch_shapes=[pltpu.VMEM((MOBA_SAMPLE_SLOTS, cp, wdt, page), F32),
                        pltpu.VMEM((MOBA_SAMPLE_SLOTS, cp, wdt, page), F32),
                        pltpu.VMEM((nrow, past), F32),
                        pltpu.SemaphoreType.DMA((MOBA_SAMPLE_SLOTS,)),
                        pltpu.SemaphoreType.DMA((MOBA_SAMPLE_SLOTS,))],
    )
    return pl.pallas_call(
        functools.partial(_moba_sample_kernel, cp),
        grid_spec=grid_spec,
        out_shape=jax.ShapeDtypeStruct((bsz, ts, MOBA_HEADS * HEAD_DIM), F32),
        compiler_params=pltpu.CompilerParams(dimension_semantics=("arbitrary",), vmem_limit_bytes=VMEM_LIMIT),
        name="moba_sample",
    )(page_table, mq_s, mk_new, mv_new, cache_kt, cache_vt)


def _prep_weights(g_attn, w_in, g_q_lora, g_kv_lora, w_uq, g_q_nope, g_q_rope, g_k_rope, w_uk, w_uv, g_k_nope,
                  g_moba_q, g_moba_k, w_oa, w_ob, w_out, g_ffn, w_group, b_group, w_expert_router,
                  b_expert_router, w_exp_gate, w_exp_up, w_exp_down):
    d = w_in.shape[0]
    ql = g_q_lora.shape[0]
    kvl = g_kv_lora.shape[0]
    nq = MOBA_HEADS * HEAD_DIM
    nkv = MOBA_KV_HEADS * HEAD_DIM
    sizes = (ql, kvl, MLA_ROPE, nq, nkv, nkv, d, d)
    assert sum(sizes) == w_in.shape[1]
    parts, o = [], 0
    for s in sizes:
        parts.append(w_in[:, o:o + s])
        o += s
    cq, ckv, kr, mq, mk, mv, ga, gb = parts
    sm = HEAD_DIM ** -0.5 * LOG2E
    sq = (MLA_NOPE + MLA_ROPE) ** -0.5 * LOG2E
    row = lambda v: v.reshape(1, -1).astype(F32)

    def w_in_layout(mqc):
        return jnp.concatenate(
            [cq, ckv, mqc, _swap_halves(mqc, HEAD_DIM), mk, _swap_halves(mk, HEAD_DIM), mv,
             _pad_lanes(kr, LANE), _pad_lanes(_swap_halves(kr, MLA_ROPE), LANE)], axis=1).astype(BF16)

    mq_gkd = mq.reshape(d, MOBA_KV_HEADS, MOBA_GROUP, HEAD_DIM).transpose(0, 2, 1, 3).reshape(d, nq)
    w = {"g_q_lora": g_q_lora, "g_kv_lora": g_kv_lora}
    w["w_in_p"] = w_in_layout(mq)
    w["w_in_s"] = w_in_layout(mq_gkd)
    w["gattn"] = row(g_attn)
    w["gql"] = row(g_q_lora)
    w["gkvl"] = row(g_kv_lora)
    w["gmq"] = row(jnp.tile(g_moba_q, MOBA_HEADS)) * sm
    w["gmqs"] = row(jnp.tile(_swap_halves(g_moba_q, HEAD_DIM), MOBA_HEADS)) * sm
    w["gmk"] = row(jnp.tile(g_moba_k, MOBA_KV_HEADS))
    w["gmks"] = row(jnp.tile(_swap_halves(g_moba_k, HEAD_DIM), MOBA_KV_HEADS))
    w["gkr"] = row(_pad_lanes(g_k_rope, LANE))
    w["gkrs"] = row(_pad_lanes(_swap_halves(g_k_rope, MLA_ROPE), LANE))
    w["g512"] = _group_matrix(GROUP_SUM_PERIOD, [(0, HEAD_DIM)], HEAD_DIM)
    w["g128"] = _group_matrix(LANE, [(0, MLA_ROPE)], LANE)
    w["g1024"] = _group_matrix(GROUP_SUM_PERIOD, [(0, MLA_NOPE), (MLA_NOPE, MLA_ROPE)], QPAD)
    tail = QPAD - MLA_NOPE - MLA_ROPE
    w["icq"] = row(jnp.tile(jnp.concatenate([jnp.full((MLA_NOPE,), 1.0 / MLA_NOPE), jnp.full((MLA_ROPE,), 1.0 / MLA_ROPE),
                                             jnp.ones((tail,))]), MLA_HEADS))
    uq = w_uq.reshape(ql, MLA_HEADS, MLA_NOPE + MLA_ROPE)
    nope, rope = uq[..., :MLA_NOPE], uq[..., MLA_NOPE:]
    zt = jnp.zeros((ql, MLA_HEADS, tail), F32)
    w["wuqa"] = jnp.concatenate([nope, rope, zt], axis=-1).reshape(ql, -1).astype(BF16)
    w["wuqb"] = jnp.concatenate([jnp.zeros_like(nope), _swap_halves(rope, MLA_ROPE), zt], axis=-1).reshape(ql, -1).astype(BF16)
    z1 = jnp.zeros((tail,), F32)
    w["gqa"] = row(jnp.tile(jnp.concatenate([g_q_nope, g_q_rope, z1]), MLA_HEADS)) * sq
    w["gqb"] = row(jnp.tile(jnp.concatenate([jnp.zeros((MLA_NOPE,), F32), _swap_halves(g_q_rope, MLA_ROPE), z1]),
                            MLA_HEADS)) * sq
    w["wuk_pad"] = _pad_lanes(w_uk, QPAD).reshape(kvl, -1).astype(BF16)
    w["gkn"] = row(jnp.tile(_pad_lanes(g_k_nope, QPAD), MLA_HEADS))
    j = jnp.arange(LANE)[:, None]
    c = jnp.arange(MLA_HEADS * QPAD)[None, :]
    w["ekr"] = ((j < MLA_ROPE) & (c % QPAD == MLA_NOPE + j)).astype(BF16)
    w["wuv"] = w_uv.reshape(kvl, -1).astype(BF16)
    w["wukt"] = w_uk.reshape(kvl, -1).T.astype(BF16)
    head_of_col = jnp.arange(MLA_HEADS * MLA_V) // MLA_V
    w["wuv_pad"] = jnp.where(head_of_col[None, None, :] == jnp.arange(MLA_HEADS)[:, None, None],
                             w_uv.reshape(1, kvl, -1), 0.0).astype(BF16)

    amax = lambda v: jnp.max(jnp.abs(v))
    w["mla_bound"] = BF16_SLACK * sq * (MLA_NOPE * amax(g_q_nope) * amax(g_k_nope) + MLA_ROPE * amax(g_q_rope) * amax(g_k_rope))
    w["moba_bound"] = BF16_SLACK * sm * HEAD_DIM * amax(g_moba_q) * amax(g_moba_k)

    w["wgate"] = jnp.concatenate([ga, gb], axis=1).astype(BF16)
    w["woa"] = w_oa.astype(BF16)
    w["wob_p"] = w_ob.astype(BF16)
    w["wob_s"] = w_ob.reshape(MOBA_KV_HEADS, MOBA_GROUP, HEAD_DIM, -1).transpose(1, 0, 2, 3).reshape(nq, -1).astype(BF16)
    w["wout"] = w_out.astype(BF16)
    w["gffn"] = row(g_ffn)
    wr = _pad_lanes(jnp.concatenate([w_expert_router, w_group], axis=1), LANE)
    w["wr_hi"], w["wr_lo"] = _split_bf16(wr)
    w["br"] = row(_pad_lanes(jnp.concatenate([b_expert_router, b_group]), LANE))
    w["wg"] = w_exp_gate.astype(BF16)
    w["wu"] = w_exp_up.astype(BF16)
    w["wd"] = w_exp_down.astype(BF16)
    return w


def _row_tile(t, want):
    return want if t % want == 0 else t


def kernel(x_prompt, x_sample, cache_ckv, cache_krope, cache_k, cache_v, page_table, g_attn, w_in, g_q_lora, g_kv_lora, w_uq, g_q_nope, g_q_rope, g_k_rope, w_uk, w_uv, g_k_nope, g_moba_q, g_moba_k, w_oa, w_ob, w_out, g_ffn, w_group, b_group, w_expert_router, b_expert_router, w_exp_gate, w_exp_up, w_exp_down):
    w = _prep_weights(g_attn, w_in, g_q_lora, g_kv_lora, w_uq, g_q_nope, g_q_rope, g_k_rope, w_uk, w_uv, g_k_nope,
                      g_moba_q, g_moba_k, w_oa, w_ob, w_out, g_ffn, w_group, b_group, w_expert_router,
                      b_expert_router, w_exp_gate, w_exp_up, w_exp_down)
    bp, t, d = x_prompt.shape
    bs, ts, _ = x_sample.shape
    npool, page = cache_ckv.shape[:2]
    past = page_table.shape[1] * page
    assert bp == 1 and t % MOBA_BLOCK == 0
    assert past % MOBA_BLOCK == 0 and ts <= LANE and ts % 8 == 0

    xp = x_prompt.reshape(t, d)
    ckv_p, kr_p, mk_p, mv_p, qt, kp, vt, mqt, mkb, mvt = _features(
        xp, jnp.arange(t, dtype=jnp.int32), w, True, _row_tile(t, ROW_TILE))
    tq = _row_tile(t, 512)
    oat = lax.cond(w["mla_bound"] <= EXP2_SAFE, functools.partial(_mla_prompt, False, tq, tq, ATTN_TILE_GROUP),
                   functools.partial(_mla_prompt, True, tq, tq, 1), qt, kp, vt)
    obt = lax.cond(w["moba_bound"] <= EXP2_SAFE, functools.partial(_moba_prompt, False, tq, ATTN_TILE_GROUP),
                   functools.partial(_moba_prompt, True, tq, 1), mqt, mkb, mvt, _block_means(mk_p))
    x1, hn, gates = _merge(xp, oat, obt, w, True, _row_tile(t, ROW_TILE))
    y_p = _moe(x1, hn, gates, w, _row_tile(t, 1024))

    xs = x_sample.reshape(bs * ts, d)
    pos_s = past + jnp.tile(jnp.arange(ts, dtype=jnp.int32), bs)
    ckv_s, kr_s, mk_s, mv_s, q_s, mq_s = _features(xs, pos_s, w, False, _row_tile(bs * ts, ROW_TILE))
    kvl = ckv_s.shape[1]
    nkv = MOBA_KV_HEADS * HEAD_DIM
    oa_s = _mla_sample(page_table, q_s.reshape(bs, ts, -1), ckv_s.reshape(bs, ts, kvl), kr_s.reshape(bs, ts, -1),
                       cache_ckv, jnp.transpose(cache_krope, (0, 2, 1)), w)
    ob_s = _moba_sample(page_table, mq_s.reshape(bs, ts, -1), mk_s.reshape(bs, ts, nkv), mv_s.reshape(bs, ts, nkv),
                        jnp.transpose(cache_k, (0, 2, 3, 1)).reshape(npool, nkv, page),
                        jnp.transpose(cache_v, (0, 2, 3, 1)).reshape(npool, nkv, page))
    x1s, hns, gates_s = _merge(xs, oa_s.reshape(bs * ts, -1), ob_s.reshape(bs * ts, -1), w, False,
                               _row_tile(bs * ts, ROW_TILE))
    y_s = _moe(x1s, hns, gates_s, w, _row_tile(bs * ts, 1024))

    return (y_p.reshape(bp, t, d), y_s.reshape(bs, ts, d),
            ckv_p.reshape(bp, t, kvl), kr_p.reshape(bp, t, MLA_ROPE),
            mk_p.reshape(bp, t, MOBA_KV_HEADS, HEAD_DIM), mv_p.reshape(bp, t, MOBA_KV_HEADS, HEAD_DIM),
            ckv_s.reshape(bs, ts, kvl), kr_s.reshape(bs, ts, MLA_ROPE),
            mk_s.reshape(bs, ts, MOBA_KV_HEADS, HEAD_DIM), mv_s.reshape(bs, ts, MOBA_KV_HEADS, HEAD_DIM))
```
